```python
import jax, jax.numpy as jnp
from jax import lax
import numpy as np

D_MODEL = 1024
BATCH = 8
SEQ = 2048
DEPTH = 4
DEC_BATCH = 32
DEC_SEQ = 2048
PAST_LEN = 128

GRID_W = 64
NA_HEADS = 8
NA_HEAD_DIM = 64
NA_WIDTH = NA_HEADS * NA_HEAD_DIM
NA_ROWS = 8
NA_COLS = 16
NA_KEY_COLS = 2 * NA_COLS
SG_WIDTH = 512
SG_GROUPS = 4
SG_GROUP_DIM = SG_WIDTH // SG_GROUPS
SG_CHUNK = 128
CV_WIDTH = 512
CV_KERNEL = 31
IN_WIDTH = 3 * NA_WIDTH + 2 * SG_WIDTH + 2 * CV_WIDTH
SPLIT_POINTS = (NA_WIDTH, 2 * NA_WIDTH, 3 * NA_WIDTH, 3 * NA_WIDTH + SG_WIDTH,
                3 * NA_WIDTH + 2 * SG_WIDTH, 3 * NA_WIDTH + 2 * SG_WIDTH + CV_WIDTH)
N_BRANCH = 3
N_EXPERTS = 16
N_GROUPS = 4
EXPERTS_PER_GROUP = N_EXPERTS // N_GROUPS
TOP_K = 2
EXPERT_FF = 512
ALPHA = (2 * DEPTH) ** 0.25
BETA = (8 * DEPTH) ** -0.25
LN_EPS = 1e-5
NEG_INF = -1e30

kernel_name = "hybrid_na_sgmlp_conv_moe_encoder"


def layer_norm(x, g, b):
    xf = x.astype(jnp.float32)
    mu = jnp.mean(xf, axis=-1, keepdims=True)
    var = jnp.mean(jnp.square(xf - mu), axis=-1, keepdims=True)
    y = (xf - mu) * lax.rsqrt(var + LN_EPS) * g.astype(jnp.float32) + b.astype(jnp.float32)
    return y.astype(x.dtype)


def neighbourhood_attention(q, k, v, rpb):
    B, L, H, Dh = q.shape
    rows = L // GRID_W
    wr = min(NA_ROWS, rows)
    n_cb = GRID_W // NA_COLS
    r = np.arange(rows)
    key_rows = np.clip(r - wr // 2, 0, rows - wr)[:, None] + np.arange(wr)
    q_cols = np.arange(GRID_W).reshape(n_cb, NA_COLS)
    key_cols = (np.clip(q_cols[:, 0] - NA_COLS // 2, 0, GRID_W - NA_KEY_COLS)[:, None]
                + np.arange(NA_KEY_COLS))
    col_start = np.clip(q_cols - NA_COLS // 2, 0, GRID_W - NA_COLS)
    kc = key_cols[:, None, :]
    col_valid = (kc >= col_start[:, :, None]) & (kc < col_start[:, :, None] + NA_COLS)
    row_off = key_rows - r[:, None] + NA_ROWS - 1
    col_off = np.clip(kc - q_cols[:, :, None] + NA_COLS - 1, 0, 2 * NA_COLS - 2)
    bias = rpb.astype(jnp.float32)[:, row_off[:, None, None, :, None], col_off[None, :, :, None, :]]
    qg = q.reshape(B, rows, n_cb, NA_COLS, H, Dh)
    kg = k.reshape(B, rows, GRID_W, H, Dh)[:, key_rows[:, None, :, None], key_cols[None, :, None, :]]
    vg = v.reshape(B, rows, GRID_W, H, Dh)[:, key_rows[:, None, :, None], key_cols[None, :, None, :]]
    s = jnp.einsum('brjqhd,brjwkhd->bhrjqwk', qg * (Dh ** -0.5), kg).astype(jnp.float32) + bias
    s = jnp.where(col_valid[:, :, None, :], s, NEG_INF)
    p = jax.nn.softmax(s, axis=(-2, -1)).astype(v.dtype)
    o = jnp.einsum('bhrjqwk,brjwkhd->brjqhd', p, vg)
    return o.reshape(B, L, H * Dh)


def token_mixer(x, w_in, b_in, w_gate, b_gate, rpb, w_a_o, sg_ln_g, sg_ln_b, w_s, b_s, w_b_o,
                conv_w, conv_b, cv_ln_g, cv_ln_b, w_c_o, w_o, b_o):
    B, L, D = x.shape
    z = x @ w_in + b_in
    q, k, v, u_sg, v_sg, glu_a, glu_g = jnp.split(z, SPLIT_POINTS, axis=-1)
    hd = (B, L, NA_HEADS, NA_HEAD_DIM)
    a = neighbourhood_attention(q.reshape(hd), k.reshape(hd), v.reshape(hd), rpb) @ w_a_o
    u = jax.nn.gelu(u_sg)
    vs = layer_norm(jax.nn.gelu(v_sg), sg_ln_g, sg_ln_b)
    vs = vs.reshape(B, L // SG_CHUNK, SG_CHUNK, SG_GROUPS, SG_GROUP_DIM)
    sv = jnp.einsum('gts,bnsgc->bntgc', w_s, vs) + b_s.T[:, :, None]
    bb = (u * sv.reshape(B, L, SG_WIDTH)) @ w_b_o
    h = glu_a * jax.nn.sigmoid(glu_g)
    h = lax.conv_general_dilated(h, conv_w[:, None, :], window_strides=(1,), padding='SAME',
                                 dimension_numbers=('NWC', 'WIO', 'NWC'),
                                 feature_group_count=CV_WIDTH) + conv_b
    c = jax.nn.silu(layer_norm(h, cv_ln_g, cv_ln_b)) @ w_c_o
    g = jax.nn.sigmoid((x @ w_gate + b_gate).astype(jnp.float32)).astype(x.dtype)
    g = g.reshape(B, L, N_BRANCH, D)
    merged = g[:, :, 0] * a + g[:, :, 1] * bb + g[:, :, 2] * c
    return merged @ w_o + b_o


def grouped_moe(x, w_router, b_router, w_e1, w_e3, w_e2):
    B, L, D = x.shape
    xt = x.reshape(B * L, D)
    s = jax.nn.sigmoid((xt @ w_router).astype(jnp.float32))
    sb = (s + b_router.astype(jnp.float32)).reshape(-1, N_GROUPS, EXPERTS_PER_GROUP)
    group_score = lax.top_k(sb, TOP_K)[0].sum(-1)
    gmask = jnp.argmax(group_score, axis=-1)[:, None] == jnp.arange(N_GROUPS)
    masked = jnp.where(gmask[:, :, None], sb, -jnp.inf).reshape(-1, N_EXPERTS)
    _, idx = lax.top_k(masked, TOP_K)
    w = jnp.take_along_axis(s, idx, axis=-1)
    w = w / jnp.sum(w, axis=-1, keepdims=True)
    gates = jnp.sum(jax.nn.one_hot(idx, N_EXPERTS, dtype=jnp.float32) * w[..., None], axis=1).astype(x.dtype)
    y = jnp.zeros_like(xt)
    for e in range(N_EXPERTS):
        h = jax.nn.silu(xt @ w_e1[e]) * (xt @ w_e3[e])
        y = y + gates[:, e:e + 1] * (h @ w_e2[e])
    return y.reshape(B, L, D)


def run_trunk(x, emb_ln_g, emb_ln_b, w_in, b_in, w_gate, b_gate, rpb, w_a_o, sg_ln_g, sg_ln_b,
              w_s, b_s, w_b_o, conv_w, conv_b, cv_ln_g, cv_ln_b, w_c_o, w_o, b_o, ln1_g, ln1_b,
              w_router, b_router, w_e1, w_e3, w_e2, ln2_g, ln2_b):
    x = layer_norm(x, emb_ln_g, emb_ln_b)
    for l in range(DEPTH):
        m = token_mixer(x, w_in[l], b_in[l], w_gate[l], b_gate[l], rpb[l], w_a_o[l], sg_ln_g[l], sg_ln_b[l],
                        w_s[l], b_s[l], w_b_o[l], conv_w[l], conv_b[l], cv_ln_g[l], cv_ln_b[l], w_c_o[l],
                        w_o[l], b_o[l])
        x = layer_norm(ALPHA * x + m, ln1_g[l], ln1_b[l])
        f = grouped_moe(x, w_router, b_router, w_e1[l], w_e3[l], w_e2[l])
        x = layer_norm(ALPHA * x + f, ln2_g[l], ln2_b[l])
    return x


def setup_inputs(seed: int = 0) -> dict:
    key = jax.random.key(seed)
    ks = iter(jax.random.split(key, 40))

    def nrm(shape, scale):
        return jax.random.normal(next(ks), shape, jnp.float32) * scale

    D = D_MODEL
    x_prompt = nrm((BATCH, SEQ, D), 1.0)
    x_sample = nrm((DEC_BATCH, DEC_SEQ, D), 1.0)
    emb_ln_g = 1.0 + nrm((D,), 0.02)
    emb_ln_b = nrm((D,), 0.02)
    col_scale = jnp.concatenate([jnp.ones((2 * NA_WIDTH,), jnp.float32),
                                 jnp.full((NA_WIDTH,), BETA, jnp.float32),
                                 jnp.ones((2 * SG_WIDTH + 2 * CV_WIDTH,), jnp.float32)])
    w_in = nrm((DEPTH, D, IN_WIDTH), D ** -0.5) * col_scale
    b_in = nrm((DEPTH, IN_WIDTH), 0.02)
    w_gate = nrm((DEPTH, D, N_BRANCH * D), D ** -0.5)
    b_gate = nrm((DEPTH, N_BRANCH * D), 0.02)
    rpb = nrm((DEPTH, NA_HEADS, 2 * NA_ROWS - 1, 2 * NA_COLS - 1), 0.1)
    w_a_o = nrm((DEPTH, NA_WIDTH, D), NA_WIDTH ** -0.5)
    sg_ln_g = 1.0 + nrm((DEPTH, SG_WIDTH), 0.02)
    sg_ln_b = nrm((DEPTH, SG_WIDTH), 0.02)
    w_s = nrm((DEPTH, SG_GROUPS, SG_CHUNK, SG_CHUNK), 0.5 * SG_CHUNK ** -0.5)
    b_s = 1.0 + nrm((DEPTH, SG_GROUPS, SG_CHUNK), 0.01)
    w_b_o = nrm((DEPTH, SG_WIDTH, D), SG_WIDTH ** -0.5)
    conv_w = nrm((DEPTH, CV_KERNEL, CV_WIDTH), CV_KERNEL ** -0.5)
    conv_b = nrm((DEPTH, CV_WIDTH), 0.02)
    cv_ln_g = 1.0 + nrm((DEPTH, CV_WIDTH), 0.02)
    cv_ln_b = nrm((DEPTH, CV_WIDTH), 0.02)
    w_c_o = nrm((DEPTH, CV_WIDTH, D), CV_WIDTH ** -0.5)
    w_o = nrm((DEPTH, D, D), BETA * D ** -0.5)
    b_o = nrm((DEPTH, D), 0.02)
    ln1_g = 1.0 + nrm((DEPTH, D), 0.02)
    ln1_b = nrm((DEPTH, D), 0.02)
    w_router = nrm((D, N_EXPERTS), D ** -0.5)
    b_router = nrm((N_EXPERTS,), 0.01)
    w_e1 = nrm((DEPTH, N_EXPERTS, D, EXPERT_FF), D ** -0.5)
    w_e3 = nrm((DEPTH, N_EXPERTS, D, EXPERT_FF), D ** -0.5)
    w_e2 = nrm((DEPTH, N_EXPERTS, EXPERT_FF, D), BETA * EXPERT_FF ** -0.5)
    ln2_g = 1.0 + nrm((DEPTH, D), 0.02)
    ln2_b = nrm((DEPTH, D), 0.02)
    return {"x_prompt": x_prompt, "x_sample": x_sample, "emb_ln_g": emb_ln_g, "emb_ln_b": emb_ln_b,
            "w_in": w_in, "b_in": b_in, "w_gate": w_gate, "b_gate": b_gate, "rpb": rpb, "w_a_o": w_a_o,
            "sg_ln_g": sg_ln_g, "sg_ln_b": sg_ln_b, "w_s": w_s, "b_s": b_s, "w_b_o": w_b_o,
            "conv_w": conv_w, "conv_b": conv_b, "cv_ln_g": cv_ln_g, "cv_ln_b": cv_ln_b, "w_c_o": w_c_o,
            "w_o": w_o, "b_o": b_o, "ln1_g": ln1_g, "ln1_b": ln1_b, "w_router": w_router,
            "b_router": b_router, "w_e1": w_e1, "w_e3": w_e3, "w_e2": w_e2, "ln2_g": ln2_g, "ln2_b": ln2_b}


def reference(x_prompt, x_sample, emb_ln_g, emb_ln_b, w_in, b_in, w_gate, b_gate, rpb, w_a_o,
              sg_ln_g, sg_ln_b, w_s, b_s, w_b_o, conv_w, conv_b, cv_ln_g, cv_ln_b, w_c_o, w_o, b_o,
              ln1_g, ln1_b, w_router, b_router, w_e1, w_e3, w_e2, ln2_g, ln2_b):
    weights = (emb_ln_g, emb_ln_b, w_in, b_in, w_gate, b_gate, rpb, w_a_o, sg_ln_g, sg_ln_b,
               w_s, b_s, w_b_o, conv_w, conv_b, cv_ln_g, cv_ln_b, w_c_o, w_o, b_o, ln1_g, ln1_b,
               w_router, b_router, w_e1, w_e3, w_e2, ln2_g, ln2_b)
    y_prompt = run_trunk(x_prompt, *weights)
    y_sample = run_trunk(x_sample, *weights)
    return (y_prompt, y_sample)
```

```python
import functools

import numpy as np
import jax
import jax.numpy as jnp
from jax import lax
from jax.experimental import pallas as pl
from jax.experimental.pallas import tpu as pltpu

F32 = jnp.float32
BF16 = jnp.bfloat16

D_MODEL = 1024
SEQ = 2048
DEPTH = 4
GRID_W = 64
ROWS = SEQ // GRID_W
NA_HEADS = 8
NA_HEAD_DIM = 64
NA_WIDTH = 512
NA_ROWS = 8
NA_COLS = 16
SG_WIDTH = 512
SG_GROUPS = 4
SG_CHUNK = 128
CV_WIDTH = 512
CV_KERNEL = 31
IN_WIDTH = 3584
N_EXPERTS = 16
N_GROUPS = 4
EXPERTS_PER_GROUP = 4
EXPERT_FF = 512
ALPHA = (2 * DEPTH) ** 0.25
LN_EPS = 1e-5
NEG_INF = -1e30

LANES = 128
TM = 512
TE = 512
TD = 2048
N_CLASSES = N_GROUPS * 6
N_CLASS_ROWS = 32
PAYLOAD = D_MODEL + LANES
COL_CLS, COL_RANK, COL_GA, COL_GB = 0, 1, 2, 3
VMEM_LIMIT = 56 * 1024 * 1024

_PAIRS = [(0, 1), (0, 2), (0, 3), (1, 2), (1, 3), (2, 3)]
_CLASS_EA = np.array([g * 4 + _PAIRS[p][0] for g in range(N_GROUPS) for p in range(6)], np.int32)
_CLASS_EB = np.array([g * 4 + _PAIRS[p][1] for g in range(N_GROUPS) for p in range(6)], np.int32)


def _ln(x, g, b):
    mu = jnp.mean(x, axis=-1, keepdims=True)
    xc = x - mu
    var = jnp.mean(xc * xc, axis=-1, keepdims=True)
    return xc * lax.rsqrt(var + LN_EPS) * g + b


def _params(*sem):
    return pltpu.CompilerParams(dimension_semantics=sem, vmem_limit_bytes=VMEM_LIMIT)


def _full(shape):
    return pl.BlockSpec(shape, lambda *_: (0,) * len(shape))


def _emb_ln_kernel(x_ref, g_ref, b_ref, o_ref):
    o_ref[...] = _ln(x_ref[...], g_ref[...], b_ref[...])


def _emb_ln(x, g, b):
    n = x.shape[0]
    return pl.pallas_call(
        _emb_ln_kernel,
        grid=(n // TM,),
        in_specs=[pl.BlockSpec((TM, D_MODEL), lambda i: (i, 0)), _full((1, D_MODEL)), _full((1, D_MODEL))],
        out_specs=pl.BlockSpec((TM, D_MODEL), lambda i: (i, 0)),
        out_shape=jax.ShapeDtypeStruct((n, D_MODEL), F32),
        compiler_params=_params("parallel"),
        name="emb_ln",
    )(x, g, b)


def _proj_kernel(x_ref, w_ref, b_ref, qkv_ref, sg_ref, cv_ref):
    xb = x_ref[:, :D_MODEL].astype(BF16)
    outs = [(qkv_ref, 0), (qkv_ref, 512), (qkv_ref, 1024), (sg_ref, 0), (sg_ref, 512), (cv_ref, 0), (cv_ref, 512)]
    for c, (ref, off) in enumerate(outs):
        z = jnp.dot(xb, w_ref[:, c * 512:(c + 1) * 512], preferred_element_type=F32) + b_ref[:, c * 512:(c + 1) * 512]
        if c == 0:
            z = z * (NA_HEAD_DIM ** -0.5)
        ref[:, off:off + 512] = z.astype(BF16)


def _proj(x, w_in, b_in):
    n = x.shape[0]
    return pl.pallas_call(
        _proj_kernel,
        grid=(n // TM,),
        in_specs=[pl.BlockSpec((TM, D_MODEL), lambda i: (i, 0)), _full((D_MODEL, IN_WIDTH)), _full((1, IN_WIDTH))],
        out_specs=[pl.BlockSpec((TM, 1536), lambda i: (i, 0)), pl.BlockSpec((TM, 1024), lambda i: (i, 0)),
                   pl.BlockSpec((TM, 1024), lambda i: (i, 0))],
        out_shape=[jax.ShapeDtypeStruct((n, 1536), BF16), jax.ShapeDtypeStruct((n, 1024), BF16),
                   jax.ShapeDtypeStruct((n, 1024), BF16)],
        compiler_params=_params("parallel"),
        name="proj",
    )(x, w_in, b_in)


N_BIAS_VARIANTS = NA_ROWS


def _attn_bias_table(rpb):
    var = np.arange(N_BIAS_VARIANTS)[:, None, None, None]
    hp = np.arange(NA_HEADS // 2)[None, :, None, None]
    i = np.arange(2 * GRID_W)[None, None, :, None]
    j = np.arange(NA_ROWS * GRID_W)[None, None, None, :]
    h = hp * 2 + i // GRID_W
    c = i % GRID_W
    w = j // GRID_W
    kc = j % GRID_W
    row_off = var + w
    col_start = np.clip(c - NA_COLS // 2, 0, GRID_W - NA_COLS)
    valid = (kc >= col_start) & (kc < col_start + NA_COLS)
    col_off = np.clip(kc - c + NA_COLS - 1, 0, 2 * NA_COLS - 2)
    shape = (N_BIAS_VARIANTS, NA_HEADS // 2, 2 * GRID_W, NA_ROWS * GRID_W)
    h, row_off, col_off, valid = (np.broadcast_to(a, shape) for a in (h, row_off, col_off, valid))
    return jnp.where(valid, rpb.astype(F32)[h, row_off, col_off], NEG_INF)


def _attn_kernel(qkv_ref, bt_ref, o_ref):
    lane = lax.broadcasted_iota(jnp.int32, (GRID_W, LANES), 1)
    first_head = lane < NA_HEAD_DIM
    n_keys = NA_ROWS * GRID_W

    def row(r, carry):
        kr0 = jnp.clip(r - NA_ROWS // 2, 0, ROWS - NA_ROWS)
        variant = kr0 - r + NA_ROWS - 1
        q0 = pl.multiple_of(r * GRID_W, GRID_W)
        k0 = pl.multiple_of(kr0 * GRID_W, GRID_W)
        for hp in range(NA_HEADS // 2):
            cols = slice(hp * LANES, (hp + 1) * LANES)
            q = qkv_ref[pl.ds(q0, GRID_W), cols]
            zero = jnp.zeros_like(q)
            qm = jnp.concatenate([jnp.where(first_head, q, zero), jnp.where(first_head, zero, q)], axis=0)
            k = qkv_ref[pl.ds(k0, n_keys), NA_WIDTH + hp * LANES:NA_WIDTH + (hp + 1) * LANES]
            v = qkv_ref[pl.ds(k0, n_keys), 2 * NA_WIDTH + hp * LANES:2 * NA_WIDTH + (hp + 1) * LANES]
            s = lax.dot_general(qm, k, (((1,), (1,)), ((), ())), preferred_element_type=F32)
            s = s + bt_ref[variant, hp]
            m = jnp.max(s, axis=-1, keepdims=True)
            p = jnp.exp(s - m)
            l = jnp.sum(p, axis=-1, keepdims=True)
            o = jnp.dot(p.astype(BF16), v, preferred_element_type=F32) / l
            o_ref[pl.ds(q0, GRID_W), cols] = jnp.where(first_head, o[:GRID_W], o[GRID_W:]).astype(BF16)
        return carry

    lax.fori_loop(0, ROWS, row, 0)


def _attention(qkv, bias_table):
    n = qkv.shape[0]
    return pl.pallas_call(
        _attn_kernel,
        grid=(n // SEQ,),
        in_specs=[pl.BlockSpec((SEQ, 3 * NA_WIDTH), lambda b: (b, 0)), _full(bias_table.shape)],
        out_specs=pl.BlockSpec((SEQ, NA_WIDTH), lambda b: (b, 0)),
        out_shape=jax.ShapeDtypeStruct((n, NA_WIDTH), BF16),
        compiler_params=_params("parallel"),
        name="attention",
    )(qkv, bias_table)


def _sg_kernel(sg_ref, ws_ref, bs_ref, g_ref, b_ref, o_ref):
    gw = SG_WIDTH // SG_GROUPS
    for ch in range(TM // SG_CHUNK):
        rows = slice(ch * SG_CHUNK, (ch + 1) * SG_CHUNK)
        u = jax.nn.gelu(sg_ref[rows, :SG_WIDTH].astype(F32))
        v = jax.nn.gelu(sg_ref[rows, SG_WIDTH:].astype(F32))
        vs = _ln(v, g_ref[...], b_ref[...]).astype(BF16)
        parts = [jnp.dot(ws_ref[g], vs[:, g * gw:(g + 1) * gw], preferred_element_type=F32)
                 for g in range(SG_GROUPS)]
        sv = jnp.concatenate(parts, axis=1) + bs_ref[...]
        o_ref[rows, :] = (u * sv).astype(BF16)


def _spatial_gate(sg, w_s, bs_full, ln_g, ln_b):
    n = sg.shape[0]
    return pl.pallas_call(
        _sg_kernel,
        grid=(n // TM,),
        in_specs=[pl.BlockSpec((TM, 2 * SG_WIDTH), lambda i: (i, 0)), _full((SG_GROUPS, SG_CHUNK, SG_CHUNK)),
                  _full((SG_CHUNK, SG_WIDTH)), _full((1, SG_WIDTH)), _full((1, SG_WIDTH))],
        out_specs=pl.BlockSpec((TM, SG_WIDTH), lambda i: (i, 0)),
        out_shape=jax.ShapeDtypeStruct((n, SG_WIDTH), BF16),
        compiler_params=_params("parallel"),
        name="spatial_gate",
    )(sg, w_s, bs_full, ln_g, ln_b)


CV_HALO = 16
CV_TT = 64
CV_CHUNK = 256


def _conv_kernel(cv_ref, w_ref, cb_ref, g_ref, b_ref, o_ref, hp_ref, acc_ref):
    zeros = jnp.zeros((CV_HALO, CV_WIDTH), F32)
    hp_ref[0:CV_HALO, :] = zeros
    hp_ref[SEQ + CV_HALO:SEQ + 2 * CV_HALO, :] = zeros

    def glu(i, carry):
        r0 = pl.multiple_of(i * CV_CHUNK, CV_CHUNK)
        a = cv_ref[pl.ds(r0, CV_CHUNK), :CV_WIDTH].astype(F32)
        g = cv_ref[pl.ds(r0, CV_CHUNK), CV_WIDTH:].astype(F32)
        hp_ref[pl.ds(r0 + CV_HALO, CV_CHUNK), :] = a * jax.nn.sigmoid(g)
        return carry

    lax.fori_loop(0, SEQ // CV_CHUNK, glu, 0)

    def tile(i, carry):
        t0 = pl.multiple_of(i * CV_TT, CV_TT)
        for cb in range(CV_WIDTH // LANES):
            cols = slice(cb * LANES, (cb + 1) * LANES)
            win = hp_ref[pl.ds(t0, CV_TT + 2 * CV_HALO), cols]
            acc = jnp.zeros((CV_TT, LANES), F32)
            for k in range(CV_KERNEL):
                off = k + CV_HALO - CV_KERNEL // 2
                acc = acc + win[off:off + CV_TT] * w_ref[k:k + 1, cols]
            acc_ref[pl.ds(t0, CV_TT), cols] = acc + cb_ref[:, cols]
        return carry

    lax.fori_loop(0, SEQ // CV_TT, tile, 0)

    def fin(i, carry):
        r0 = pl.multiple_of(i * CV_CHUNK, CV_CHUNK)
        y = _ln(acc_ref[pl.ds(r0, CV_CHUNK), :], g_ref[...], b_ref[...])
        o_ref[pl.ds(r0, CV_CHUNK), :] = (y * jax.nn.sigmoid(y)).astype(BF16)
        return carry

    lax.fori_loop(0, SEQ // CV_CHUNK, fin, 0)


def _conv_module(cv, conv_w, conv_b, ln_g, ln_b):
    n = cv.shape[0]
    return pl.pallas_call(
        _conv_kernel,
        grid=(n // SEQ,),
        in_specs=[pl.BlockSpec((SEQ, 2 * CV_WIDTH), lambda b: (b, 0)), _full((CV_KERNEL, CV_WIDTH)),
                  _full((1, CV_WIDTH)), _full((1, CV_WIDTH)), _full((1, CV_WIDTH))],
        out_specs=pl.BlockSpec((SEQ, CV_WIDTH), lambda b: (b, 0)),
        out_shape=jax.ShapeDtypeStruct((n, CV_WIDTH), BF16),
        scratch_shapes=[pltpu.VMEM((SEQ + 2 * CV_HALO, CV_WIDTH), F32), pltpu.VMEM((SEQ, CV_WIDTH), F32)],
        compiler_params=_params("parallel"),
        name="conv_module",
    )(cv, conv_w, conv_b, ln_g, ln_b)


def _first_argmax(vals):
    best, idx = vals[0], jnp.zeros(vals[0].shape, jnp.int32)
    for j in range(1, len(vals)):
        upd = vals[j] > best
        best = jnp.where(upd, vals[j], best)
        idx = jnp.where(upd, j, idx)
    return best, idx


def _select(idx, vals):
    out = vals[0]
    for j in range(1, len(vals)):
        out = jnp.where(idx == j, vals[j], out)
    return out


def _route(logits, b_router):
    s = jax.nn.sigmoid(logits)
    sb = s + b_router
    s_rows = [s[e:e + 1, :] for e in range(N_EXPERTS)]
    sb_rows = [sb[e:e + 1, :] for e in range(N_EXPERTS)]
    scores, firsts, seconds = [], [], []
    for g in range(N_GROUPS):
        vals = sb_rows[g * EXPERTS_PER_GROUP:(g + 1) * EXPERTS_PER_GROUP]
        m1, i1 = _first_argmax(vals)
        rest = [jnp.where(i1 == j, -jnp.inf, vals[j]) for j in range(EXPERTS_PER_GROUP)]
        m2, i2 = _first_argmax(rest)
        scores.append(m1 + m2)
        firsts.append(i1)
        seconds.append(i2)
    _, grp = _first_argmax(scores)
    i1 = _select(grp, firsts)
    i2 = _select(grp, seconds)
    lo = jnp.minimum(i1, i2)
    hi = jnp.maximum(i1, i2)
    pair = jnp.where(lo == 0, hi - 1, jnp.where(lo == 1, hi + 1, 5))
    cls = grp * 6 + pair
    s_grp = [_select(grp, [s_rows[g * EXPERTS_PER_GROUP + j] for g in range(N_GROUPS)])
             for j in range(EXPERTS_PER_GROUP)]
    w_lo = _select(lo, s_grp)
    w_hi = _select(hi, s_grp)
    tot = w_lo + w_hi
    return cls, w_lo / tot, w_hi / tot


def _merge_kernel(x_ref, a_ref, sg_ref, cv_ref, wg_ref, bg_ref, wa_ref, wb_ref, wc_ref, wo_ref, bo_ref,
                  g1_ref, b1_ref, wr_ref, br_ref, x1p_ref, route_ref, counts_ref, carry_ref):
    @pl.when(pl.program_id(0) == 0)
    def _():
        carry_ref[...] = jnp.zeros_like(carry_ref)

    x = x_ref[...]
    xb = x.astype(BF16)
    merged = None
    for i, (branch_ref, w_ref) in enumerate(((a_ref, wa_ref), (sg_ref, wb_ref), (cv_ref, wc_ref))):
        cols = slice(i * D_MODEL, (i + 1) * D_MODEL)
        gate = jax.nn.sigmoid(jnp.dot(xb, wg_ref[:, cols], preferred_element_type=F32) + bg_ref[:, cols])
        term = gate * jnp.dot(branch_ref[...], w_ref[...], preferred_element_type=F32)
        merged = term if merged is None else merged + term
    m = jnp.dot(merged.astype(BF16), wo_ref[...], preferred_element_type=F32) + bo_ref[...]
    x1 = _ln(ALPHA * x + m, g1_ref[...], b1_ref[...])
    x1p_ref[:, :D_MODEL] = x1

    nt = (((1,), (1,)), ((), ()))
    x_hi = x1.astype(BF16)
    x_lo = (x1 - x_hi.astype(F32)).astype(BF16)
    wr = wr_ref[...]
    w_hi = wr.astype(BF16)
    w_lo = (wr - w_hi.astype(F32)).astype(BF16)
    logits = (lax.dot_general(w_hi, x_hi, nt, preferred_element_type=F32)
              + lax.dot_general(w_hi, x_lo, nt, preferred_element_type=F32)
              + lax.dot_general(w_lo, x_hi, nt, preferred_element_type=F32))
    cls, ga, gb = _route(logits, br_ref[...])

    onehot = (lax.broadcasted_iota(jnp.int32, (N_CLASS_ROWS, TM), 0) == cls)
    oh_f = jnp.where(onehot, 1.0, 0.0)
    upper = (lax.broadcasted_iota(jnp.int32, (TM, TM), 0) <= lax.broadcasted_iota(jnp.int32, (TM, TM), 1))
    prefix = jnp.dot(oh_f.astype(BF16), jnp.where(upper, 1.0, 0.0).astype(BF16), preferred_element_type=F32)
    carry = carry_ref[:, 0:1]
    rank = jnp.sum(oh_f * (prefix - 1.0 + carry), axis=0, keepdims=True)
    new_carry = carry_ref[...] + jnp.sum(oh_f, axis=1, keepdims=True)
    carry_ref[...] = new_carry
    counts_ref[...] = new_carry

    rows = jnp.concatenate([cls.astype(F32), rank, ga, gb, jnp.zeros((4, TM), F32)], axis=0)
    route_ref[...] = rows
    x1p_ref[:, D_MODEL:] = jnp.concatenate([rows, jnp.zeros((LANES - 8, TM), F32)], axis=0).T


def _merge(x, a, sg, cv, w_gate, b_gate, w_a_o, w_b_o, w_c_o, w_o, b_o, ln_g, ln_b, w_router_t, b_router):
    n = x.shape[0]
    tok = lambda w: pl.BlockSpec((TM, w), lambda i: (i, 0))
    return pl.pallas_call(
        _merge_kernel,
        grid=(n // TM,),
        in_specs=[tok(D_MODEL), tok(NA_WIDTH), tok(SG_WIDTH), tok(CV_WIDTH),
                  _full((D_MODEL, 3 * D_MODEL)), _full((1, 3 * D_MODEL)),
                  _full((NA_WIDTH, D_MODEL)), _full((SG_WIDTH, D_MODEL)), _full((CV_WIDTH, D_MODEL)),
                  _full((D_MODEL, D_MODEL)), _full((1, D_MODEL)), _full((1, D_MODEL)), _full((1, D_MODEL)),
                  _full((N_EXPERTS, D_MODEL)), _full((N_EXPERTS, 1))],
        out_specs=[tok(PAYLOAD), pl.BlockSpec((None, 8, TM), lambda i: (i, 0, 0)),
                   _full((N_CLASS_ROWS, LANES))],
        out_shape=[jax.ShapeDtypeStruct((n, PAYLOAD), F32), jax.ShapeDtypeStruct((n // TM, 8, TM), F32),
                   jax.ShapeDtypeStruct((N_CLASS_ROWS, LANES), F32)],
        scratch_shapes=[pltpu.VMEM((N_CLASS_ROWS, LANES), F32)],
        compiler_params=_params("arbitrary"),
        name="merge_ln1_route",
    )(x, a, sg, cv, w_gate, b_gate, w_a_o, w_b_o, w_c_o, w_o, b_o, ln_g, ln_b, w_router_t, b_router)


def _row_copy(src_hbm, dst_hbm, src_row, dst_row, sem):
    return pltpu.make_async_copy(src_hbm.at[pl.ds(src_row, 1)], dst_hbm.at[pl.ds(dst_row, 1)], sem)


def _scatter_rows_kernel(dest_ref, src_hbm, init_hbm, dst_hbm, sem):
    del init_hbm
    base = pl.program_id(0) * TD

    def issue(t, carry):
        _row_copy(src_hbm, dst_hbm, base + t, dest_ref[0, 0, t], sem).start()
        return carry

    def drain(t, carry):
        _row_copy(src_hbm, dst_hbm, 0, 0, sem).wait()
        return carry

    lax.fori_loop(0, TD, issue, 0)
    lax.fori_loop(0, TD, drain, 0)


def _gather_rows_kernel(dest_ref, src_hbm, dst_hbm, sem):
    base = pl.program_id(0) * TD

    def issue(t, carry):
        _row_copy(src_hbm, dst_hbm, dest_ref[0, 0, t], base + t, sem).start()
        return carry

    def drain(t, carry):
        _row_copy(src_hbm, dst_hbm, 0, 0, sem).wait()
        return carry

    lax.fori_loop(0, TD, issue, 0)
    lax.fori_loop(0, TD, drain, 0)


def _dest_spec():
    return pl.BlockSpec((1, 1, TD), lambda i: (i, 0, 0), memory_space=pltpu.SMEM)


def _scatter_rows(dest, src, n_rows_out):
    n, width = src.shape
    init = jnp.zeros((n_rows_out, width), src.dtype)
    return pl.pallas_call(
        _scatter_rows_kernel,
        grid=(n // TD,),
        in_specs=[_dest_spec(), pl.BlockSpec(memory_space=pl.ANY), pl.BlockSpec(memory_space=pl.ANY)],
        out_specs=pl.BlockSpec(memory_space=pl.ANY),
        out_shape=jax.ShapeDtypeStruct((n_rows_out, width), src.dtype),
        scratch_shapes=[pltpu.SemaphoreType.DMA(())],
        input_output_aliases={2: 0},
        compiler_params=pltpu.CompilerParams(dimension_semantics=("arbitrary",)),
        name="scatter_rows",
    )(dest.reshape(n // TD, 1, TD), src, init)


def _gather_rows(dest, src):
    n = dest.shape[0]
    width = src.shape[1]
    return pl.pallas_call(
        _gather_rows_kernel,
        grid=(n // TD,),
        in_specs=[_dest_spec(), pl.BlockSpec(memory_space=pl.ANY)],
        out_specs=pl.BlockSpec(memory_space=pl.ANY),
        out_shape=jax.ShapeDtypeStruct((n, width), src.dtype),
        scratch_shapes=[pltpu.SemaphoreType.DMA(())],
        compiler_params=pltpu.CompilerParams(dimension_semantics=("arbitrary",)),
        name="gather_rows",
    )(dest.reshape(n // TD, 1, TD), src)


def _expert_kernel(ea_ref, eb_ref, src_ref, valid_ref, xs_ref, w1a_ref, w3a_ref, w1b_ref, w3b_ref,
                   w2a_ref, w2b_ref, g_ref, b_ref, o_ref):
    del ea_ref, eb_ref, src_ref

    @pl.when(valid_ref[pl.program_id(0)] == 1)
    def _():
        x = xs_ref[:, :D_MODEL]
        xb = x.astype(BF16)
        ga = xs_ref[:, D_MODEL + COL_GA:D_MODEL + COL_GA + 1]
        gb = xs_ref[:, D_MODEL + COL_GB:D_MODEL + COL_GB + 1]

        def hidden(w1_ref, w3_ref, gate):
            a = jnp.dot(xb, w1_ref[...], preferred_element_type=F32)
            b = jnp.dot(xb, w3_ref[...], preferred_element_type=F32)
            return (a * jax.nn.sigmoid(a) * b * gate).astype(BF16)

        y = (jnp.dot(hidden(w1a_ref, w3a_ref, ga), w2a_ref[...], preferred_element_type=F32)
             + jnp.dot(hidden(w1b_ref, w3b_ref, gb), w2b_ref[...], preferred_element_type=F32))
        o_ref[...] = _ln(ALPHA * x + y, g_ref[...], b_ref[...])


def _experts(tile_ea, tile_eb, tile_src, tile_valid, xs, w_e1, w_e3, w_e2, ln_g, ln_b):
    n_tiles = xs.shape[0] // TE
    up = lambda which: pl.BlockSpec((None, D_MODEL, EXPERT_FF), lambda i, ea, eb, src, valid: ((ea, eb)[which][i], 0, 0))
    down = lambda which: pl.BlockSpec((None, EXPERT_FF, D_MODEL), lambda i, ea, eb, src, valid: ((ea, eb)[which][i], 0, 0))
    vec = pl.BlockSpec((1, D_MODEL), lambda i, *_: (0, 0))
    grid_spec = pltpu.PrefetchScalarGridSpec(
        num_scalar_prefetch=4,
        grid=(n_tiles,),
        in_specs=[pl.BlockSpec((TE, PAYLOAD), lambda i, ea, eb, src, valid: (src[i], 0)),
                  up(0), up(0), up(1), up(1), down(0), down(1), vec, vec],
        out_specs=pl.BlockSpec((TE, D_MODEL), lambda i, *_: (i, 0)),
    )
    return pl.pallas_call(
        _expert_kernel,
        grid_spec=grid_spec,
        out_shape=jax.ShapeDtypeStruct((n_tiles * TE, D_MODEL), F32),
        compiler_params=_params("arbitrary"),
        name="expert_pair_ln2",
    )(tile_ea, tile_eb, tile_src, tile_valid, xs, w_e1, w_e3, w_e1, w_e3, w_e2, w_e2, ln_g, ln_b)


def _routing_tables(route, counts, n):
    n_tiles = n // TE + N_CLASSES
    cls = route[:, COL_CLS, :].reshape(n).astype(jnp.int32)
    rank = route[:, COL_RANK, :].reshape(n).astype(jnp.int32)
    cnt = counts[:N_CLASSES, 0].astype(jnp.int32)
    padded = ((cnt + TE - 1) // TE) * TE
    ends = jnp.cumsum(padded)
    starts = ends - padded
    dest = starts[cls] + rank
    tile_start = jnp.arange(n_tiles, dtype=jnp.int32) * TE
    tile_valid = (tile_start < ends[-1]).astype(jnp.int32)
    last_tile = ends[-1] // TE - 1
    tile_src = jnp.minimum(jnp.arange(n_tiles, dtype=jnp.int32), last_tile)
    tile_cls = jnp.searchsorted(ends, tile_src * TE, side="right").astype(jnp.int32)
    tile_cls = jnp.minimum(tile_cls, N_CLASSES - 1)
    return dest, jnp.asarray(_CLASS_EA)[tile_cls], jnp.asarray(_CLASS_EB)[tile_cls], tile_src, tile_valid


def _trunk(x, emb_ln_g, emb_ln_b, w_in, b_in, w_gate, b_gate, rpb, w_a_o, sg_ln_g, sg_ln_b, w_s, b_s, w_b_o,
           conv_w, conv_b, cv_ln_g, cv_ln_b, w_c_o, w_o, b_o, ln1_g, ln1_b, w_router, b_router,
           w_e1, w_e3, w_e2, ln2_g, ln2_b):
    n = x.shape[0]
    row = lambda a: a.reshape(1, -1).astype(F32)
    bf = lambda a: a.astype(BF16)
    w_router_t = w_router.T.astype(F32)
    b_router_c = b_router.reshape(N_EXPERTS, 1).astype(F32)
    n_sorted = (n // TE + N_CLASSES) * TE

    x = _emb_ln(x, row(emb_ln_g), row(emb_ln_b))
    for l in range(DEPTH):
        qkv, sg, cv = _proj(x, bf(w_in[l]), row(b_in[l]))
        a = _attention(qkv, _attn_bias_table(rpb[l]))
        bs_full = jnp.repeat(b_s[l].T.astype(F32), SG_WIDTH // SG_GROUPS, axis=1)
        sgo = _spatial_gate(sg, bf(w_s[l]), bs_full, row(sg_ln_g[l]), row(sg_ln_b[l]))
        cvo = _conv_module(cv, conv_w[l].astype(F32), row(conv_b[l]), row(cv_ln_g[l]), row(cv_ln_b[l]))
        x1p, route, counts = _merge(x, a, sgo, cvo, bf(w_gate[l]), row(b_gate[l]), bf(w_a_o[l]), bf(w_b_o[l]),
                                    bf(w_c_o[l]), bf(w_o[l]), row(b_o[l]), row(ln1_g[l]), row(ln1_b[l]),
                                    w_router_t, b_router_c)
        dest, tile_ea, tile_eb, tile_src, tile_valid = _routing_tables(route, counts, n)
        xs = _scatter_rows(dest, x1p, n_sorted)
        x2s = _experts(tile_ea, tile_eb, tile_src, tile_valid, xs, bf(w_e1[l]), bf(w_e3[l]), bf(w_e2[l]),
                       row(ln2_g[l]), row(ln2_b[l]))
        x = _gather_rows(dest, x2s)
    return x


def kernel(x_prompt, x_sample, emb_ln_g, emb_ln_b, w_in, b_in, w_gate, b_gate, rpb, w_a_o, sg_ln_g, sg_ln_b,
           w_s, b_s, w_b_o, conv_w, conv_b, cv_ln_g, cv_ln_b, w_c_o, w_o, b_o, ln1_g, ln1_b, w_router,
           b_router, w_e1, w_e3, w_e2, ln2_g, ln2_b):
    weights = (emb_ln_g, emb_ln_b, w_in, b_in, w_gate, b_gate, rpb, w_a_o, sg_ln_g, sg_ln_b, w_s, b_s, w_b_o,
               conv_w, conv_b, cv_ln_g, cv_ln_b, w_c_o, w_o, b_o, ln1_g, ln1_b, w_router, b_router,
               w_e1, w_e3, w_e2, ln2_g, ln2_b)
    n_prompt = x_prompt.shape[0] * x_prompt.shape[1]
    x = jnp.concatenate([x_prompt.reshape(-1, D_MODEL), x_sample.reshape(-1, D_MODEL)], axis=0)
    y = _trunk(x, *weights)
    return y[:n_prompt].reshape(x_prompt.shape), y[n_prompt:].reshape(x_sample.shape)
```

```python
import functools

import numpy as np
import jax
import jax.numpy as jnp
from jax import lax
from jax.experimental import pallas as pl
from jax.experimental.pallas import tpu as pltpu

F32 = jnp.float32
BF16 = jnp.bfloat16

D_MODEL = 1024
SEQ = 2048
DEPTH = 4
GRID_W = 64
ROWS = SEQ // GRID_W
NA_HEADS = 8
NA_HEAD_DIM = 64
NA_WIDTH = 512
NA_ROWS = 8
NA_COLS = 16
SG_WIDTH = 512
SG_GROUPS = 4
SG_CHUNK = 128
CV_WIDTH = 512
CV_KERNEL = 31
IN_WIDTH = 3584
N_EXPERTS = 16
N_GROUPS = 4
EXPERTS_PER_GROUP = 4
EXPERT_FF = 512
ALPHA = (2 * DEPTH) ** 0.25
LN_EPS = 1e-5
NEG_INF = -1e30

LANES = 128
TM = 512
TE = 512
N_CLASSES = N_GROUPS * 6
N_CLASS_ROWS = 32
PAYLOAD = D_MODEL + LANES
COL_CLS, COL_RANK, COL_GA, COL_GB = 0, 1, 2, 3
VMEM_LIMIT = 56 * 1024 * 1024

_PAIRS = [(0, 1), (0, 2), (0, 3), (1, 2), (1, 3), (2, 3)]
_CLASS_EA = np.array([g * 4 + _PAIRS[p][0] for g in range(N_GROUPS) for p in range(6)], np.int32)
_CLASS_EB = np.array([g * 4 + _PAIRS[p][1] for g in range(N_GROUPS) for p in range(6)], np.int32)


def _ln(x, g, b):
    mu = jnp.mean(x, axis=-1, keepdims=True)
    xc = x - mu
    var = jnp.mean(xc * xc, axis=-1, keepdims=True)
    return xc * lax.rsqrt(var + LN_EPS) * g + b


def _params(*sem):
    return pltpu.CompilerParams(dimension_semantics=sem, vmem_limit_bytes=VMEM_LIMIT)


def _full(shape):
    return pl.BlockSpec(shape, lambda *_: (0,) * len(shape))


def _emb_ln_kernel(x_ref, g_ref, b_ref, o_ref):
    o_ref[...] = _ln(x_ref[...], g_ref[...], b_ref[...])


def _emb_ln(x, g, b):
    n = x.shape[0]
    return pl.pallas_call(
        _emb_ln_kernel,
        grid=(n // TM,),
        in_specs=[pl.BlockSpec((TM, D_MODEL), lambda i: (i, 0)), _full((1, D_MODEL)), _full((1, D_MODEL))],
        out_specs=pl.BlockSpec((TM, D_MODEL), lambda i: (i, 0)),
        out_shape=jax.ShapeDtypeStruct((n, D_MODEL), F32),
        compiler_params=_params("parallel"),
        name="emb_ln",
    )(x, g, b)


def _project(x, w_ref, b_ref, qkv_ref, sg_ref, cv_ref):
    xb = x.astype(BF16)
    outs = [(qkv_ref, 0), (qkv_ref, 512), (qkv_ref, 1024), (sg_ref, 0), (sg_ref, 512), (cv_ref, 0), (cv_ref, 512)]
    for c, (ref, off) in enumerate(outs):
        z = jnp.dot(xb, w_ref[:, c * 512:(c + 1) * 512], preferred_element_type=F32) + b_ref[:, c * 512:(c + 1) * 512]
        if c == 0:
            z = z * (NA_HEAD_DIM ** -0.5)
        ref[:, off:off + 512] = z.astype(BF16)


def _proj_kernel(x_ref, w_ref, b_ref, qkv_ref, sg_ref, cv_ref):
    _project(x_ref[...], w_ref, b_ref, qkv_ref, sg_ref, cv_ref)


def _proj_outs(n):
    tok = lambda w: pl.BlockSpec((TM, w), lambda i: (i, 0))
    specs = [tok(3 * NA_WIDTH), tok(2 * SG_WIDTH), tok(2 * CV_WIDTH)]
    shapes = [jax.ShapeDtypeStruct((n, 3 * NA_WIDTH), BF16), jax.ShapeDtypeStruct((n, 2 * SG_WIDTH), BF16),
              jax.ShapeDtypeStruct((n, 2 * CV_WIDTH), BF16)]
    return specs, shapes


def _proj(x, w_in, b_in):
    n = x.shape[0]
    out_specs, out_shape = _proj_outs(n)
    return pl.pallas_call(
        _proj_kernel,
        grid=(n // TM,),
        in_specs=[pl.BlockSpec((TM, D_MODEL), lambda i: (i, 0)), _full((D_MODEL, IN_WIDTH)), _full((1, IN_WIDTH))],
        out_specs=out_specs,
        out_shape=out_shape,
        compiler_params=_params("parallel"),
        name="proj",
    )(x, w_in, b_in)


def _gather_tile_copy(src_hbm, buf, slot, sem):
    return pltpu.make_async_copy(src_hbm.at[pl.ds(0, TM)], buf.at[slot], sem.at[slot])


def _start_row_gather(dest_ref, src_hbm, buf, slot, sem):
    def issue(t, carry):
        pltpu.make_async_copy(src_hbm.at[pl.ds(dest_ref[0, 0, t], 1)], buf.at[slot, pl.ds(t, 1)],
                              sem.at[slot]).start()
        return carry

    lax.fori_loop(0, TM, issue, 0, unroll=8)


def _gather_proj_kernel(dest_ref, dest_next_ref, src_hbm, w_ref, b_ref, x_ref, qkv_ref, sg_ref, cv_ref, buf, sem):
    i = pl.program_id(0)
    slot = i % 2

    @pl.when(i == 0)
    def _():
        _start_row_gather(dest_ref, src_hbm, buf, 0, sem)

    @pl.when(i + 1 < pl.num_programs(0))
    def _():
        _start_row_gather(dest_next_ref, src_hbm, buf, 1 - slot, sem)

    _gather_tile_copy(src_hbm, buf, slot, sem).wait()
    x = buf[slot]
    x_ref[...] = x
    _project(x, w_ref, b_ref, qkv_ref, sg_ref, cv_ref)


def _gather_proj(dest, x_sorted, w_in, b_in):
    n = dest.shape[0]
    n_tiles = n // TM
    out_specs, out_shape = _proj_outs(n)
    dest3 = dest.reshape(n_tiles, 1, TM)
    smem = lambda index_map: pl.BlockSpec((1, 1, TM), index_map, memory_space=pltpu.SMEM)
    return pl.pallas_call(
        _gather_proj_kernel,
        grid=(n_tiles,),
        in_specs=[smem(lambda i: (i, 0, 0)), smem(lambda i: (jnp.minimum(i + 1, n_tiles - 1), 0, 0)),
                  pl.BlockSpec(memory_space=pl.ANY), _full((D_MODEL, IN_WIDTH)), _full((1, IN_WIDTH))],
        out_specs=[pl.BlockSpec((TM, D_MODEL), lambda i: (i, 0))] + out_specs,
        out_shape=[jax.ShapeDtypeStruct((n, D_MODEL), F32)] + out_shape,
        scratch_shapes=[pltpu.VMEM((2, TM, D_MODEL), F32), pltpu.SemaphoreType.DMA((2,))],
        compiler_params=_params("arbitrary"),
        name="gather_proj",
    )(dest3, dest3, x_sorted, w_in, b_in)


N_BIAS_VARIANTS = NA_ROWS


def _attn_bias_table(rpb):
    c = np.arange(GRID_W)[:, None]
    kc = np.arange(GRID_W)[None, :]
    col_start = np.clip(c - NA_COLS // 2, 0, GRID_W - NA_COLS)
    valid = (kc >= col_start) & (kc < col_start + NA_COLS)
    col_off = kc - c + NA_COLS - 1
    n_off = 2 * NA_COLS - 1
    onehot = ((col_off[None] == np.arange(n_off)[:, None, None]) & valid[None]).astype(np.float32)
    t = jnp.einsum("hdo,ock->hdck", rpb.astype(F32), jnp.asarray(onehot), precision=lax.Precision.HIGHEST)
    t = t + jnp.asarray(np.where(valid, 0.0, NEG_INF).astype(np.float32))
    bt = jnp.stack([t[:, v:v + NA_ROWS] for v in range(N_BIAS_VARIANTS)])
    bt = bt.transpose(0, 1, 3, 2, 4)
    return bt.reshape(N_BIAS_VARIANTS, NA_HEADS // 2, 2 * GRID_W, NA_ROWS * GRID_W)


def _attn_kernel(qkv_ref, bt_ref, o_ref):
    lane = lax.broadcasted_iota(jnp.int32, (GRID_W, LANES), 1)
    first_head = lane < NA_HEAD_DIM
    n_keys = NA_ROWS * GRID_W

    def row(r, carry):
        kr0 = jnp.clip(r - NA_ROWS // 2, 0, ROWS - NA_ROWS)
        variant = kr0 - r + NA_ROWS - 1
        q0 = pl.multiple_of(r * GRID_W, GRID_W)
        k0 = pl.multiple_of(kr0 * GRID_W, GRID_W)
        for hp in range(NA_HEADS // 2):
            cols = slice(hp * LANES, (hp + 1) * LANES)
            q = qkv_ref[pl.ds(q0, GRID_W), cols]
            zero = jnp.zeros_like(q)
            qm = jnp.concatenate([jnp.where(first_head, q, zero), jnp.where(first_head, zero, q)], axis=0)
            k = qkv_ref[pl.ds(k0, n_keys), NA_WIDTH + hp * LANES:NA_WIDTH + (hp + 1) * LANES]
            v = qkv_ref[pl.ds(k0, n_keys), 2 * NA_WIDTH + hp * LANES:2 * NA_WIDTH + (hp + 1) * LANES]
            s = lax.dot_general(qm, k, (((1,), (1,)), ((), ())), preferred_element_type=F32)
            s = s + bt_ref[variant, hp]
            m = jnp.max(s, axis=-1, keepdims=True)
            p = jnp.exp(s - m)
            l = jnp.sum(p, axis=-1, keepdims=True)
            o = jnp.dot(p.astype(BF16), v, preferred_element_type=F32) / l
            o_ref[pl.ds(q0, GRID_W), cols] = jnp.where(first_head, o[:GRID_W], o[GRID_W:]).astype(BF16)
        return carry

    lax.fori_loop(0, ROWS, row, 0)


def _attention(qkv, bias_table):
    n = qkv.shape[0]
    return pl.pallas_call(
        _attn_kernel,
        grid=(n // SEQ,),
        in_specs=[pl.BlockSpec((SEQ, 3 * NA_WIDTH), lambda b: (b, 0)), _full(bias_table.shape)],
        out_specs=pl.BlockSpec((SEQ, NA_WIDTH), lambda b: (b, 0)),
        out_shape=jax.ShapeDtypeStruct((n, NA_WIDTH), BF16),
        compiler_params=_params("parallel"),
        name="attention",
    )(qkv, bias_table)


def _sg_kernel(sg_ref, ws_ref, bs_ref, g_ref, b_ref, o_ref):
    gw = SG_WIDTH // SG_GROUPS
    for ch in range(TM // SG_CHUNK):
        rows = slice(ch * SG_CHUNK, (ch + 1) * SG_CHUNK)
        u = jax.nn.gelu(sg_ref[rows, :SG_WIDTH].astype(F32))
        v = jax.nn.gelu(sg_ref[rows, SG_WIDTH:].astype(F32))
        vs = _ln(v, g_ref[...], b_ref[...]).astype(BF16)
        parts = [jnp.dot(ws_ref[g], vs[:, g * gw:(g + 1) * gw], preferred_element_type=F32)
                 for g in range(SG_GROUPS)]
        sv = jnp.concatenate(parts, axis=1) + bs_ref[...]
        o_ref[rows, :] = (u * sv).astype(BF16)


def _spatial_gate(sg, w_s, bs_full, ln_g, ln_b):
    n = sg.shape[0]
    return pl.pallas_call(
        _sg_kernel,
        grid=(n // TM,),
        in_specs=[pl.BlockSpec((TM, 2 * SG_WIDTH), lambda i: (i, 0)), _full((SG_GROUPS, SG_CHUNK, SG_CHUNK)),
                  _full((SG_CHUNK, SG_WIDTH)), _full((1, SG_WIDTH)), _full((1, SG_WIDTH))],
        out_specs=pl.BlockSpec((TM, SG_WIDTH), lambda i: (i, 0)),
        out_shape=jax.ShapeDtypeStruct((n, SG_WIDTH), BF16),
        compiler_params=_params("parallel"),
        name="spatial_gate",
    )(sg, w_s, bs_full, ln_g, ln_b)


CV_HALO = 16
CV_TT = 64
CV_CHUNK = 256


def _conv_kernel(cv_ref, w_ref, cb_ref, g_ref, b_ref, o_ref, hp_ref, acc_ref):
    zeros = jnp.zeros((CV_HALO, CV_WIDTH), F32)
    hp_ref[0:CV_HALO, :] = zeros
    hp_ref[SEQ + CV_HALO:SEQ + 2 * CV_HALO, :] = zeros

    def glu(i, carry):
        r0 = pl.multiple_of(i * CV_CHUNK, CV_CHUNK)
        a = cv_ref[pl.ds(r0, CV_CHUNK), :CV_WIDTH].astype(F32)
        g = cv_ref[pl.ds(r0, CV_CHUNK), CV_WIDTH:].astype(F32)
        hp_ref[pl.ds(r0 + CV_HALO, CV_CHUNK), :] = a * jax.nn.sigmoid(g)
        return carry

    lax.fori_loop(0, SEQ // CV_CHUNK, glu, 0)

    def tile(i, carry):
        t0 = pl.multiple_of(i * CV_TT, CV_TT)
        for cb in range(CV_WIDTH // LANES):
            cols = slice(cb * LANES, (cb + 1) * LANES)
            win = hp_ref[pl.ds(t0, CV_TT + 2 * CV_HALO), cols]
            acc = jnp.zeros((CV_TT, LANES), F32)
            for k in range(CV_KERNEL):
                off = k + CV_HALO - CV_KERNEL // 2
                acc = acc + win[off:off + CV_TT] * w_ref[k:k + 1, cols]
            acc_ref[pl.ds(t0, CV_TT), cols] = acc + cb_ref[:, cols]
        return carry

    lax.fori_loop(0, SEQ // CV_TT, tile, 0)

    def fin(i, carry):
        r0 = pl.multiple_of(i * CV_CHUNK, CV_CHUNK)
        y = _ln(acc_ref[pl.ds(r0, CV_CHUNK), :], g_ref[...], b_ref[...])
        o_ref[pl.ds(r0, CV_CHUNK), :] = (y * jax.nn.sigmoid(y)).astype(BF16)
        return carry

    lax.fori_loop(0, SEQ // CV_CHUNK, fin, 0)


def _conv_module(cv, conv_w, conv_b, ln_g, ln_b):
    n = cv.shape[0]
    return pl.pallas_call(
        _conv_kernel,
        grid=(n // SEQ,),
        in_specs=[pl.BlockSpec((SEQ, 2 * CV_WIDTH), lambda b: (b, 0)), _full((CV_KERNEL, CV_WIDTH)),
                  _full((1, CV_WIDTH)), _full((1, CV_WIDTH)), _full((1, CV_WIDTH))],
        out_specs=pl.BlockSpec((SEQ, CV_WIDTH), lambda b: (b, 0)),
        out_shape=jax.ShapeDtypeStruct((n, CV_WIDTH), BF16),
        scratch_shapes=[pltpu.VMEM((SEQ + 2 * CV_HALO, CV_WIDTH), F32), pltpu.VMEM((SEQ, CV_WIDTH), F32)],
        compiler_params=_params("parallel"),
        name="conv_module",
    )(cv, conv_w, conv_b, ln_g, ln_b)


def _first_argmax(vals):
    best, idx = vals[0], jnp.zeros(vals[0].shape, jnp.int32)
    for j in range(1, len(vals)):
        upd = vals[j] > best
        best = jnp.where(upd, vals[j], best)
        idx = jnp.where(upd, j, idx)
    return best, idx


def _select(idx, vals):
    out = vals[0]
    for j in range(1, len(vals)):
        out = jnp.where(idx == j, vals[j], out)
    return out


def _route(logits, b_router):
    s = jax.nn.sigmoid(logits)
    sb = s + b_router
    s_rows = [s[e:e + 1, :] for e in range(N_EXPERTS)]
    sb_rows = [sb[e:e + 1, :] for e in range(N_EXPERTS)]
    scores, firsts, seconds = [], [], []
    for g in range(N_GROUPS):
        vals = sb_rows[g * EXPERTS_PER_GROUP:(g + 1) * EXPERTS_PER_GROUP]
        m1, i1 = _first_argmax(vals)
        rest = [jnp.where(i1 == j, -jnp.inf, vals[j]) for j in range(EXPERTS_PER_GROUP)]
        m2, i2 = _first_argmax(rest)
        scores.append(m1 + m2)
        firsts.append(i1)
        seconds.append(i2)
    _, grp = _first_argmax(scores)
    i1 = _select(grp, firsts)
    i2 = _select(grp, seconds)
    lo = jnp.minimum(i1, i2)
    hi = jnp.maximum(i1, i2)
    pair = jnp.where(lo == 0, hi - 1, jnp.where(lo == 1, hi + 1, 5))
    cls = grp * 6 + pair
    s_grp = [_select(grp, [s_rows[g * EXPERTS_PER_GROUP + j] for g in range(N_GROUPS)])
             for j in range(EXPERTS_PER_GROUP)]
    w_lo = _select(lo, s_grp)
    w_hi = _select(hi, s_grp)
    tot = w_lo + w_hi
    return cls, w_lo / tot, w_hi / tot


def _merge_kernel(x_ref, a_ref, sg_ref, cv_ref, wg_ref, bg_ref, wa_ref, wb_ref, wc_ref, wo_ref, bo_ref,
                  g1_ref, b1_ref, wr_ref, br_ref, x1p_ref, route_ref, counts_ref, carry_ref):
    @pl.when(pl.program_id(0) == 0)
    def _():
        carry_ref[...] = jnp.zeros_like(carry_ref)

    x = x_ref[...]
    xb = x.astype(BF16)
    merged = None
    for i, (branch_ref, w_ref) in enumerate(((a_ref, wa_ref), (sg_ref, wb_ref), (cv_ref, wc_ref))):
        cols = slice(i * D_MODEL, (i + 1) * D_MODEL)
        gate = jax.nn.sigmoid(jnp.dot(xb, wg_ref[:, cols], preferred_element_type=F32) + bg_ref[:, cols])
        term = gate * jnp.dot(branch_ref[...], w_ref[...], preferred_element_type=F32)
        merged = term if merged is None else merged + term
    m = jnp.dot(merged.astype(BF16), wo_ref[...], preferred_element_type=F32) + bo_ref[...]
    x1 = _ln(ALPHA * x + m, g1_ref[...], b1_ref[...])
    x1p_ref[:, :D_MODEL] = x1

    nt = (((1,), (1,)), ((), ()))
    x_hi = x1.astype(BF16)
    x_lo = (x1 - x_hi.astype(F32)).astype(BF16)
    wr = wr_ref[...]
    w_hi = wr.astype(BF16)
    w_lo = (wr - w_hi.astype(F32)).astype(BF16)
    logits = (lax.dot_general(w_hi, x_hi, nt, preferred_element_type=F32)
              + lax.dot_general(w_hi, x_lo, nt, preferred_element_type=F32)
              + lax.dot_general(w_lo, x_hi, nt, preferred_element_type=F32))
    cls, ga, gb = _route(logits, br_ref[...])

    onehot = (lax.broadcasted_iota(jnp.int32, (N_CLASS_ROWS, TM), 0) == cls)
    oh_f = jnp.where(onehot, 1.0, 0.0)
    upper = (lax.broadcasted_iota(jnp.int32, (TM, TM), 0) <= lax.broadcasted_iota(jnp.int32, (TM, TM), 1))
    prefix = jnp.dot(oh_f.astype(BF16), jnp.where(upper, 1.0, 0.0).astype(BF16), preferred_element_type=F32)
    carry = carry_ref[:, 0:1]
    rank = jnp.sum(oh_f * (prefix - 1.0 + carry), axis=0, keepdims=True)
    new_carry = carry_ref[...] + jnp.sum(oh_f, axis=1, keepdims=True)
    carry_ref[...] = new_carry
    counts_ref[...] = new_carry

    rows = jnp.concatenate([cls.astype(F32), rank, ga, gb, jnp.zeros((4, TM), F32)], axis=0)
    route_ref[...] = rows
    x1p_ref[:, D_MODEL:] = jnp.concatenate([rows, jnp.zeros((LANES - 8, TM), F32)], axis=0).T


def _merge(x, a, sg, cv, w_gate, b_gate, w_a_o, w_b_o, w_c_o, w_o, b_o, ln_g, ln_b, w_router_t, b_router):
    n = x.shape[0]
    tok = lambda w: pl.BlockSpec((TM, w), lambda i: (i, 0))
    return pl.pallas_call(
        _merge_kernel,
        grid=(n // TM,),
        in_specs=[tok(D_MODEL), tok(NA_WIDTH), tok(SG_WIDTH), tok(CV_WIDTH),
                  _full((D_MODEL, 3 * D_MODEL)), _full((1, 3 * D_MODEL)),
                  _full((NA_WIDTH, D_MODEL)), _full((SG_WIDTH, D_MODEL)), _full((CV_WIDTH, D_MODEL)),
                  _full((D_MODEL, D_MODEL)), _full((1, D_MODEL)), _full((1, D_MODEL)), _full((1, D_MODEL)),
                  _full((N_EXPERTS, D_MODEL)), _full((N_EXPERTS, 1))],
        out_specs=[tok(PAYLOAD), pl.BlockSpec((None, 8, TM), lambda i: (i, 0, 0)),
                   _full((N_CLASS_ROWS, LANES))],
        out_shape=[jax.ShapeDtypeStruct((n, PAYLOAD), F32), jax.ShapeDtypeStruct((n // TM, 8, TM), F32),
                   jax.ShapeDtypeStruct((N_CLASS_ROWS, LANES), F32)],
        scratch_shapes=[pltpu.VMEM((N_CLASS_ROWS, LANES), F32)],
        compiler_params=_params("arbitrary"),
        name="merge_ln1_route",
    )(x, a, sg, cv, w_gate, b_gate, w_a_o, w_b_o, w_c_o, w_o, b_o, ln_g, ln_b, w_router_t, b_router)


def _dest_spec():
    return pl.BlockSpec((1, 1, TM), lambda i: (i, 0, 0), memory_space=pltpu.SMEM)


def _scatter_rows_kernel(dest_ref, src_ref, init_hbm, dst_hbm, sem):
    del init_hbm

    def issue(t, carry):
        pltpu.make_async_copy(src_ref.at[pl.ds(t, 1)], dst_hbm.at[pl.ds(dest_ref[0, 0, t], 1)], sem).start()
        return carry

    lax.fori_loop(0, TM, issue, 0, unroll=8)
    pltpu.make_async_copy(src_ref, dst_hbm.at[pl.ds(0, TM)], sem).wait()


def _scatter_rows(dest, src, n_rows_out):
    n, width = src.shape
    init = jnp.zeros((n_rows_out, width), src.dtype)
    return pl.pallas_call(
        _scatter_rows_kernel,
        grid=(n // TM,),
        in_specs=[_dest_spec(), pl.BlockSpec((TM, width), lambda i: (i, 0)), pl.BlockSpec(memory_space=pl.ANY)],
        out_specs=pl.BlockSpec(memory_space=pl.ANY),
        out_shape=jax.ShapeDtypeStruct((n_rows_out, width), src.dtype),
        scratch_shapes=[pltpu.SemaphoreType.DMA(())],
        input_output_aliases={2: 0},
        compiler_params=_params("arbitrary"),
        name="scatter_rows",
    )(dest.reshape(n // TM, 1, TM), src, init)


def _gather_rows_kernel(dest_ref, src_hbm, o_ref, sem):
    def issue(t, carry):
        pltpu.make_async_copy(src_hbm.at[pl.ds(dest_ref[0, 0, t], 1)], o_ref.at[pl.ds(t, 1)], sem).start()
        return carry

    lax.fori_loop(0, TM, issue, 0, unroll=8)
    pltpu.make_async_copy(src_hbm.at[pl.ds(0, TM)], o_ref, sem).wait()


def _gather_rows(dest, src):
    n = dest.shape[0]
    width = src.shape[1]
    return pl.pallas_call(
        _gather_rows_kernel,
        grid=(n // TM,),
        in_specs=[_dest_spec(), pl.BlockSpec(memory_space=pl.ANY)],
        out_specs=pl.BlockSpec((TM, width), lambda i: (i, 0)),
        out_shape=jax.ShapeDtypeStruct((n, width), src.dtype),
        scratch_shapes=[pltpu.SemaphoreType.DMA(())],
        compiler_params=_params("arbitrary"),
        name="gather_rows",
    )(dest.reshape(n // TM, 1, TM), src)


def _expert_kernel(ea_ref, eb_ref, src_ref, valid_ref, xs_ref, w1a_ref, w3a_ref, w1b_ref, w3b_ref,
                   w2a_ref, w2b_ref, g_ref, b_ref, o_ref):
    del ea_ref, eb_ref, src_ref

    @pl.when(valid_ref[pl.program_id(0)] == 1)
    def _():
        x = xs_ref[:, :D_MODEL]
        xb = x.astype(BF16)
        ga = xs_ref[:, D_MODEL + COL_GA:D_MODEL + COL_GA + 1]
        gb = xs_ref[:, D_MODEL + COL_GB:D_MODEL + COL_GB + 1]

        def hidden(w1_ref, w3_ref, gate):
            a = jnp.dot(xb, w1_ref[...], preferred_element_type=F32)
            b = jnp.dot(xb, w3_ref[...], preferred_element_type=F32)
            return (a * jax.nn.sigmoid(a) * b * gate).astype(BF16)

        y = (jnp.dot(hidden(w1a_ref, w3a_ref, ga), w2a_ref[...], preferred_element_type=F32)
             + jnp.dot(hidden(w1b_ref, w3b_ref, gb), w2b_ref[...], preferred_element_type=F32))
        o_ref[...] = _ln(ALPHA * x + y, g_ref[...], b_ref[...])


def _experts(tile_ea, tile_eb, tile_src, tile_valid, xs, w_e1, w_e3, w_e2, ln_g, ln_b):
    n_tiles = xs.shape[0] // TE
    up = lambda which: pl.BlockSpec((None, D_MODEL, EXPERT_FF), lambda i, ea, eb, src, valid: ((ea, eb)[which][i], 0, 0))
    down = lambda which: pl.BlockSpec((None, EXPERT_FF, D_MODEL), lambda i, ea, eb, src, valid: ((ea, eb)[which][i], 0, 0))
    vec = pl.BlockSpec((1, D_MODEL), lambda i, *_: (0, 0))
    grid_spec = pltpu.PrefetchScalarGridSpec(
        num_scalar_prefetch=4,
        grid=(n_tiles,),
        in_specs=[pl.BlockSpec((TE, PAYLOAD), lambda i, ea, eb, src, valid: (src[i], 0)),
                  up(0), up(0), up(1), up(1), down(0), down(1), vec, vec],
        out_specs=pl.BlockSpec((TE, D_MODEL), lambda i, *_: (i, 0)),
    )
    return pl.pallas_call(
        _expert_kernel,
        grid_spec=grid_spec,
        out_shape=jax.ShapeDtypeStruct((n_tiles * TE, D_MODEL), F32),
        compiler_params=_params("arbitrary"),
        name="expert_pair_ln2",
    )(tile_ea, tile_eb, tile_src, tile_valid, xs, w_e1, w_e3, w_e1, w_e3, w_e2, w_e2, ln_g, ln_b)


def _routing_tables(route, counts, n):
    n_tiles = n // TE + N_CLASSES
    cls = route[:, COL_CLS, :].reshape(n).astype(jnp.int32)
    rank = route[:, COL_RANK, :].reshape(n).astype(jnp.int32)
    cnt = counts[:N_CLASSES, 0].astype(jnp.int32)
    padded = ((cnt + TE - 1) // TE) * TE
    ends = jnp.cumsum(padded)
    starts = ends - padded
    dest = starts[cls] + rank
    tile_start = jnp.arange(n_tiles, dtype=jnp.int32) * TE
    tile_valid = (tile_start < ends[-1]).astype(jnp.int32)
    last_tile = ends[-1] // TE - 1
    tile_src = jnp.minimum(jnp.arange(n_tiles, dtype=jnp.int32), last_tile)
    tile_cls = jnp.searchsorted(ends, tile_src * TE, side="right").astype(jnp.int32)
    tile_cls = jnp.minimum(tile_cls, N_CLASSES - 1)
    return dest, jnp.asarray(_CLASS_EA)[tile_cls], jnp.asarray(_CLASS_EB)[tile_cls], tile_src, tile_valid


def _trunk(x, emb_ln_g, emb_ln_b, w_in, b_in, w_gate, b_gate, rpb, w_a_o, sg_ln_g, sg_ln_b, w_s, b_s, w_b_o,
           conv_w, conv_b, cv_ln_g, cv_ln_b, w_c_o, w_o, b_o, ln1_g, ln1_b, w_router, b_router,
           w_e1, w_e3, w_e2, ln2_g, ln2_b):
    n = x.shape[0]
    row = lambda a: a.reshape(1, -1).astype(F32)
    bf = lambda a: a.astype(BF16)
    w_router_t = w_router.T.astype(F32)
    b_router_c = b_router.reshape(N_EXPERTS, 1).astype(F32)
    n_sorted = (n // TE + N_CLASSES) * TE

    x = _emb_ln(x, row(emb_ln_g), row(emb_ln_b))
    dest = x2s = None
    for l in range(DEPTH):
        if l == 0:
            qkv, sg, cv = _proj(x, bf(w_in[l]), row(b_in[l]))
        else:
            x, qkv, sg, cv = _gather_proj(dest, x2s, bf(w_in[l]), row(b_in[l]))
        a = _attention(qkv, _attn_bias_table(rpb[l]))
        bs_full = jnp.repeat(b_s[l].T.astype(F32), SG_WIDTH // SG_GROUPS, axis=1)
        sgo = _spatial_gate(sg, bf(w_s[l]), bs_full, row(sg_ln_g[l]), row(sg_ln_b[l]))
        cvo = _conv_module(cv, conv_w[l].astype(F32), row(conv_b[l]), row(cv_ln_g[l]), row(cv_ln_b[l]))
        x1p, route, counts = _merge(x, a, sgo, cvo, bf(w_gate[l]), row(b_gate[l]), bf(w_a_o[l]), bf(w_b_o[l]),
                                    bf(w_c_o[l]), bf(w_o[l]), row(b_o[l]), row(ln1_g[l]), row(ln1_b[l]),
                                    w_router_t, b_router_c)
        dest, tile_ea, tile_eb, tile_src, tile_valid = _routing_tables(route, counts, n)
        xs = _scatter_rows(dest, x1p, n_sorted)
        x2s = _experts(tile_ea, tile_eb, tile_src, tile_valid, xs, bf(w_e1[l]), bf(w_e3[l]), bf(w_e2[l]),
                       row(ln2_g[l]), row(ln2_b[l]))
    return dest, x2s


def kernel(x_prompt, x_sample, emb_ln_g, emb_ln_b, w_in, b_in, w_gate, b_gate, rpb, w_a_o, sg_ln_g, sg_ln_b,
           w_s, b_s, w_b_o, conv_w, conv_b, cv_ln_g, cv_ln_b, w_c_o, w_o, b_o, ln1_g, ln1_b, w_router,
           b_router, w_e1, w_e3, w_e2, ln2_g, ln2_b):
    weights = (emb_ln_g, emb_ln_b, w_in, b_in, w_gate, b_gate, rpb, w_a_o, sg_ln_g, sg_ln_b, w_s, b_s, w_b_o,
               conv_w, conv_b, cv_ln_g, cv_ln_b, w_c_o, w_o, b_o, ln1_g, ln1_b, w_router, b_router,
               w_e1, w_e3, w_e2, ln2_g, ln2_b)
    n_prompt = x_prompt.shape[0] * x_prompt.shape[1]
    x = jnp.concatenate([x_prompt.reshape(-1, D_MODEL), x_sample.reshape(-1, D_MODEL)], axis=0)
    dest, x_sorted = _trunk(x, *weights)
    y_prompt = _gather_rows(dest[:n_prompt], x_sorted)
    y_sample = _gather_rows(dest[n_prompt:], x_sorted)
    return y_prompt.reshape(x_prompt.shape), y_sample.reshape(x_sample.shape)
```

```python
import functools

import numpy as np
import jax
import jax.numpy as jnp
from jax import lax
from jax.experimental import pallas as pl
from jax.experimental.pallas import tpu as pltpu

F32 = jnp.float32
BF16 = jnp.bfloat16

D_MODEL = 1024
SEQ = 2048
DEPTH = 4
GRID_W = 64
ROWS = SEQ // GRID_W
NA_HEADS = 8
NA_HEAD_DIM = 64
NA_WIDTH = 512
NA_ROWS = 8
NA_COLS = 16
SG_WIDTH = 512
SG_GROUPS = 4
SG_CHUNK = 128
CV_WIDTH = 512
CV_KERNEL = 31
IN_WIDTH = 3584
N_EXPERTS = 16
N_GROUPS = 4
EXPERTS_PER_GROUP = 4
EXPERT_FF = 512
ALPHA = (2 * DEPTH) ** 0.25
LN_EPS = 1e-5
NEG_INF = -1e30

LANES = 128
SUBLANES = 8
TM = 512
TE = 512
N_CLASSES = N_GROUPS * 6
N_CLASS_ROWS = 32
PAYLOAD = D_MODEL + LANES
COL_CLS, COL_RANK, COL_GA, COL_GB = 0, 1, 2, 3
VMEM_LIMIT = 56 * 1024 * 1024

_PAIRS = [(0, 1), (0, 2), (0, 3), (1, 2), (1, 3), (2, 3)]
_CLASS_EA = np.array([g * 4 + _PAIRS[p][0] for g in range(N_GROUPS) for p in range(6)], np.int32)
_CLASS_EB = np.array([g * 4 + _PAIRS[p][1] for g in range(N_GROUPS) for p in range(6)], np.int32)


def _ln(x, g, b):
    mu = jnp.mean(x, axis=-1, keepdims=True)
    xc = x - mu
    var = jnp.mean(xc * xc, axis=-1, keepdims=True)
    return xc * lax.rsqrt(var + LN_EPS) * g + b


def _params(*sem):
    return pltpu.CompilerParams(dimension_semantics=sem, vmem_limit_bytes=VMEM_LIMIT)


def _full(shape):
    return pl.BlockSpec(shape, lambda *_: (0,) * len(shape))


def _project(x, w_ref, b_ref, qkv_ref, sg_ref, cv_ref):
    xb = x.astype(BF16)
    outs = [(qkv_ref, 0), (qkv_ref, 512), (qkv_ref, 1024), (sg_ref, 0), (sg_ref, 512), (cv_ref, 0), (cv_ref, 512)]
    for c, (ref, off) in enumerate(outs):
        z = jnp.dot(xb, w_ref[:, c * 512:(c + 1) * 512], preferred_element_type=F32) + b_ref[:, c * 512:(c + 1) * 512]
        if c == 0:
            z = z * (NA_HEAD_DIM ** -0.5)
        ref[:, off:off + 512] = z.astype(BF16)


def _proj_outs(n):
    tok = lambda w: pl.BlockSpec((TM, w), lambda i: (i, 0))
    specs = [tok(D_MODEL), tok(3 * NA_WIDTH), tok(2 * SG_WIDTH), tok(2 * CV_WIDTH)]
    shapes = [jax.ShapeDtypeStruct((n, D_MODEL), F32), jax.ShapeDtypeStruct((n, 3 * NA_WIDTH), BF16),
              jax.ShapeDtypeStruct((n, 2 * SG_WIDTH), BF16), jax.ShapeDtypeStruct((n, 2 * CV_WIDTH), BF16)]
    return specs, shapes


def _embed_proj_kernel(xa_ref, xb_ref, g_ref, b_ref, w_ref, bias_ref, x_ref, qkv_ref, sg_ref, cv_ref, *, n_a_tiles):
    raw = jnp.where(pl.program_id(0) < n_a_tiles, xa_ref[...], xb_ref[...])
    x = _ln(raw, g_ref[...], b_ref[...])
    x_ref[...] = x
    _project(x, w_ref, bias_ref, qkv_ref, sg_ref, cv_ref)


def _embed_proj(x_a, x_b, ln_g, ln_b, w_in, b_in):
    n_a_tiles, n_b_tiles = x_a.shape[0] // TM, x_b.shape[0] // TM
    n = x_a.shape[0] + x_b.shape[0]
    out_specs, out_shape = _proj_outs(n)
    return pl.pallas_call(
        functools.partial(_embed_proj_kernel, n_a_tiles=n_a_tiles),
        grid=(n_a_tiles + n_b_tiles,),
        in_specs=[pl.BlockSpec((TM, D_MODEL), lambda i: (jnp.minimum(i, n_a_tiles - 1), 0)),
                  pl.BlockSpec((TM, D_MODEL), lambda i: (jnp.maximum(i - n_a_tiles, 0), 0)),
                  _full((1, D_MODEL)), _full((1, D_MODEL)), _full((D_MODEL, IN_WIDTH)), _full((1, IN_WIDTH))],
        out_specs=out_specs,
        out_shape=out_shape,
        compiler_params=_params("arbitrary"),
        name="embed_proj",
    )(x_a, x_b, ln_g, ln_b, w_in, b_in)


def _gather_tile_copy(src_hbm, buf, slot, sem):
    return pltpu.make_async_copy(src_hbm.at[pl.ds(0, TM)], buf.at[slot], sem.at[slot])


def _start_row_gather(dest_ref, src_hbm, buf, slot, sem):
    def issue(t, carry):
        pltpu.make_async_copy(src_hbm.at[pl.ds(dest_ref[0, 0, t], 1)], buf.at[slot, pl.ds(t, 1)],
                              sem.at[slot]).start()
        return carry

    lax.fori_loop(0, TM, issue, 0, unroll=8)


def _gather_proj_kernel(dest_ref, dest_next_ref, src_hbm, w_ref, b_ref, x_ref, qkv_ref, sg_ref, cv_ref, buf, sem):
    i = pl.program_id(0)
    slot = i % 2

    @pl.when(i == 0)
    def _():
        _start_row_gather(dest_ref, src_hbm, buf, 0, sem)

    @pl.when(i + 1 < pl.num_programs(0))
    def _():
        _start_row_gather(dest_next_ref, src_hbm, buf, 1 - slot, sem)

    _gather_tile_copy(src_hbm, buf, slot, sem).wait()
    x = buf[slot]
    x_ref[...] = x
    _project(x, w_ref, b_ref, qkv_ref, sg_ref, cv_ref)


def _gather_proj(dest, x_sorted, w_in, b_in):
    n = dest.shape[0]
    n_tiles = n // TM
    out_specs, out_shape = _proj_outs(n)
    dest3 = dest.reshape(n_tiles, 1, TM)
    smem = lambda index_map: pl.BlockSpec((1, 1, TM), index_map, memory_space=pltpu.SMEM)
    return pl.pallas_call(
        _gather_proj_kernel,
        grid=(n_tiles,),
        in_specs=[smem(lambda i: (i, 0, 0)), smem(lambda i: (jnp.minimum(i + 1, n_tiles - 1), 0, 0)),
                  pl.BlockSpec(memory_space=pl.ANY), _full((D_MODEL, IN_WIDTH)), _full((1, IN_WIDTH))],
        out_specs=out_specs,
        out_shape=out_shape,
        scratch_shapes=[pltpu.VMEM((2, TM, D_MODEL), F32), pltpu.SemaphoreType.DMA((2,))],
        compiler_params=_params("arbitrary"),
        name="gather_proj",
    )(dest3, dest3, x_sorted, w_in, b_in)


N_BIAS_VARIANTS = NA_ROWS


def _attn_bias_table(rpb):
    c = np.arange(GRID_W)[:, None]
    kc = np.arange(GRID_W)[None, :]
    col_start = np.clip(c - NA_COLS // 2, 0, GRID_W - NA_COLS)
    valid = (kc >= col_start) & (kc < col_start + NA_COLS)
    col_off = kc - c + NA_COLS - 1
    n_off = 2 * NA_COLS - 1
    onehot = ((col_off[None] == np.arange(n_off)[:, None, None]) & valid[None]).astype(np.float32)
    t = jnp.einsum("hdo,ock->hdck", rpb.astype(F32), jnp.asarray(onehot), precision=lax.Precision.HIGHEST)
    t = t + jnp.asarray(np.where(valid, 0.0, NEG_INF).astype(np.float32))
    bt = jnp.stack([t[:, v:v + NA_ROWS] for v in range(N_BIAS_VARIANTS)])
    bt = bt.transpose(0, 1, 3, 2, 4)
    return bt.reshape(N_BIAS_VARIANTS, NA_HEADS // 2, 2 * GRID_W, NA_ROWS * GRID_W)


def _attn_kernel(qkv_ref, bt_ref, o_ref, s0_ref, s1_ref, m0_ref, m1_ref):
    lane = lax.broadcasted_iota(jnp.int32, (GRID_W, LANES), 1)
    first_head = lane < NA_HEAD_DIM
    n_keys = NA_ROWS * GRID_W
    n_pairs = NA_HEADS // 2
    slots = ((s0_ref, m0_ref), (s1_ref, m1_ref))

    def window(r):
        kr0 = jnp.clip(r - NA_ROWS // 2, 0, ROWS - NA_ROWS)
        return kr0 - r + NA_ROWS - 1, pl.multiple_of(r * GRID_W, GRID_W), pl.multiple_of(kr0 * GRID_W, GRID_W)

    def scores(r, hp):
        variant, q0, k0 = window(r)
        s_ref, m_ref = slots[hp % 2]
        q = qkv_ref[pl.ds(q0, GRID_W), hp * LANES:(hp + 1) * LANES]
        zero = jnp.zeros_like(q)
        qm = jnp.concatenate([jnp.where(first_head, q, zero), jnp.where(first_head, zero, q)], axis=0)
        k = qkv_ref[pl.ds(k0, n_keys), NA_WIDTH + hp * LANES:NA_WIDTH + (hp + 1) * LANES]
        s = lax.dot_general(qm, k, (((1,), (1,)), ((), ())), preferred_element_type=F32) + bt_ref[variant, hp]
        s_ref[...] = s
        m_ref[...] = jnp.broadcast_to(jnp.max(s, axis=-1, keepdims=True), m_ref.shape)

    def attend(r, hp):
        _, q0, k0 = window(r)
        s_ref, m_ref = slots[hp % 2]
        v = qkv_ref[pl.ds(k0, n_keys), 2 * NA_WIDTH + hp * LANES:2 * NA_WIDTH + (hp + 1) * LANES]
        p = jnp.exp(s_ref[...] - m_ref[:, 0:1])
        l = jnp.sum(p, axis=-1, keepdims=True)
        o = jnp.dot(p.astype(BF16), v, preferred_element_type=F32) / l
        o_ref[pl.ds(q0, GRID_W), hp * LANES:(hp + 1) * LANES] = (
            jnp.where(first_head, o[:GRID_W], o[GRID_W:]).astype(BF16))

    def row(r, carry):
        for hp in range(n_pairs):
            if hp + 1 < n_pairs:
                scores(r, hp + 1)
            else:
                scores(jnp.minimum(r + 1, ROWS - 1), 0)
            attend(r, hp)
        return carry

    scores(0, 0)
    lax.fori_loop(0, ROWS, row, 0, unroll=2)


def _attention(qkv, bias_table):
    n = qkv.shape[0]
    return pl.pallas_call(
        _attn_kernel,
        grid=(n // SEQ,),
        in_specs=[pl.BlockSpec((SEQ, 3 * NA_WIDTH), lambda b: (b, 0)), _full(bias_table.shape)],
        out_specs=pl.BlockSpec((SEQ, NA_WIDTH), lambda b: (b, 0)),
        out_shape=jax.ShapeDtypeStruct((n, NA_WIDTH), BF16),
        scratch_shapes=[pltpu.VMEM((2 * GRID_W, NA_ROWS * GRID_W), F32)] * 2 + [pltpu.VMEM((2 * GRID_W, LANES), F32)] * 2,
        compiler_params=_params("parallel"),
        name="attention",
    )(qkv, bias_table)


def _sg_kernel(sg_ref, ws_ref, bs_ref, g_ref, b_ref, o_ref):
    gw = SG_WIDTH // SG_GROUPS
    for ch in range(TM // SG_CHUNK):
        rows = slice(ch * SG_CHUNK, (ch + 1) * SG_CHUNK)
        u = jax.nn.gelu(sg_ref[rows, :SG_WIDTH].astype(F32))
        v = jax.nn.gelu(sg_ref[rows, SG_WIDTH:].astype(F32))
        vs = _ln(v, g_ref[...], b_ref[...]).astype(BF16)
        parts = [jnp.dot(ws_ref[g], vs[:, g * gw:(g + 1) * gw], preferred_element_type=F32)
                 for g in range(SG_GROUPS)]
        sv = jnp.concatenate(parts, axis=1) + bs_ref[...]
        o_ref[rows, :] = (u * sv).astype(BF16)


def _spatial_gate(sg, w_s, bs_full, ln_g, ln_b):
    n = sg.shape[0]
    return pl.pallas_call(
        _sg_kernel,
        grid=(n // TM,),
        in_specs=[pl.BlockSpec((TM, 2 * SG_WIDTH), lambda i: (i, 0)), _full((SG_GROUPS, SG_CHUNK, SG_CHUNK)),
                  _full((SG_CHUNK, SG_WIDTH)), _full((1, SG_WIDTH)), _full((1, SG_WIDTH))],
        out_specs=pl.BlockSpec((TM, SG_WIDTH), lambda i: (i, 0)),
        out_shape=jax.ShapeDtypeStruct((n, SG_WIDTH), BF16),
        compiler_params=_params("parallel"),
        name="spatial_gate",
    )(sg, w_s, bs_full, ln_g, ln_b)


CV_HALO = 16
CV_TT = 64
CV_CHUNK = 256
CV_SHIFTED = CV_TT + 2 * CV_HALO - SUBLANES


def _conv_kernel(cv_ref, w_ref, cb_ref, g_ref, b_ref, o_ref, hp_ref, acc_ref, sh_ref):
    zeros = jnp.zeros((CV_HALO, CV_WIDTH), F32)
    hp_ref[0:CV_HALO, :] = zeros
    hp_ref[SEQ + CV_HALO:SEQ + 2 * CV_HALO, :] = zeros

    def glu(i, carry):
        r0 = pl.multiple_of(i * CV_CHUNK, CV_CHUNK)
        a = cv_ref[pl.ds(r0, CV_CHUNK), :CV_WIDTH].astype(F32)
        g = cv_ref[pl.ds(r0, CV_CHUNK), CV_WIDTH:].astype(F32)
        hp_ref[pl.ds(r0 + CV_HALO, CV_CHUNK), :] = a * jax.nn.sigmoid(g)
        return carry

    lax.fori_loop(0, SEQ // CV_CHUNK, glu, 0)

    def tile(i, carry):
        t0 = pl.multiple_of(i * CV_TT, CV_TT)
        for cb in range(CV_WIDTH // LANES):
            cols = slice(cb * LANES, (cb + 1) * LANES)
            win = hp_ref[pl.ds(t0, CV_TT + 2 * CV_HALO), cols]
            for res in range(1, SUBLANES):
                sh_ref[cb, res] = win[res:res + CV_SHIFTED]
            acc = jnp.zeros((CV_TT, LANES), F32)
            for res in range(SUBLANES):
                for a in range((2 * CV_HALO) // SUBLANES):
                    k = a * SUBLANES + res - (CV_HALO - CV_KERNEL // 2)
                    if 0 <= k < CV_KERNEL:
                        rows = slice(a * SUBLANES, a * SUBLANES + CV_TT)
                        tap = win[rows] if res == 0 else sh_ref[cb, res, rows]
                        acc = acc + tap * w_ref[k:k + 1, cols]
            acc_ref[pl.ds(t0, CV_TT), cols] = acc + cb_ref[:, cols]
        return carry

    lax.fori_loop(0, SEQ // CV_TT, tile, 0)

    def fin(i, carry):
        r0 = pl.multiple_of(i * CV_CHUNK, CV_CHUNK)
        y = _ln(acc_ref[pl.ds(r0, CV_CHUNK), :], g_ref[...], b_ref[...])
        o_ref[pl.ds(r0, CV_CHUNK), :] = (y * jax.nn.sigmoid(y)).astype(BF16)
        return carry

    lax.fori_loop(0, SEQ // CV_CHUNK, fin, 0)


def _conv_module(cv, conv_w, conv_b, ln_g, ln_b):
    n = cv.shape[0]
    return pl.pallas_call(
        _conv_kernel,
        grid=(n // SEQ,),
        in_specs=[pl.BlockSpec((SEQ, 2 * CV_WIDTH), lambda b: (b, 0)), _full((CV_KERNEL, CV_WIDTH)),
                  _full((1, CV_WIDTH)), _full((1, CV_WIDTH)), _full((1, CV_WIDTH))],
        out_specs=pl.BlockSpec((SEQ, CV_WIDTH), lambda b: (b, 0)),
        out_shape=jax.ShapeDtypeStruct((n, CV_WIDTH), BF16),
        scratch_shapes=[pltpu.VMEM((SEQ + 2 * CV_HALO, CV_WIDTH), F32), pltpu.VMEM((SEQ, CV_WIDTH), F32),
                        pltpu.VMEM((CV_WIDTH // LANES, SUBLANES, CV_SHIFTED, LANES), F32)],
        compiler_params=_params("parallel"),
        name="conv_module",
    )(cv, conv_w, conv_b, ln_g, ln_b)


def _first_argmax(vals):
    best, idx = vals[0], jnp.zeros(vals[0].shape, jnp.int32)
    for j in range(1, len(vals)):
        upd = vals[j] > best
        best = jnp.where(upd, vals[j], best)
        idx = jnp.where(upd, j, idx)
    return best, idx


def _select(idx, vals):
    out = vals[0]
    for j in range(1, len(vals)):
        out = jnp.where(idx == j, vals[j], out)
    return out


def _route(logits, b_router):
    s = jax.nn.sigmoid(logits)
    sb = s + b_router
    s_rows = [s[e:e + 1, :] for e in range(N_EXPERTS)]
    sb_rows = [sb[e:e + 1, :] for e in range(N_EXPERTS)]
    scores, firsts, seconds = [], [], []
    for g in range(N_GROUPS):
        vals = sb_rows[g * EXPERTS_PER_GROUP:(g + 1) * EXPERTS_PER_GROUP]
        m1, i1 = _first_argmax(vals)
        rest = [jnp.where(i1 == j, -jnp.inf, vals[j]) for j in range(EXPERTS_PER_GROUP)]
        m2, i2 = _first_argmax(rest)
        scores.append(m1 + m2)
        firsts.append(i1)
        seconds.append(i2)
    _, grp = _first_argmax(scores)
    i1 = _select(grp, firsts)
    i2 = _select(grp, seconds)
    lo = jnp.minimum(i1, i2)
    hi = jnp.maximum(i1, i2)
    pair = jnp.where(lo == 0, hi - 1, jnp.where(lo == 1, hi + 1, 5))
    cls = grp * 6 + pair
    s_grp = [_select(grp, [s_rows[g * EXPERTS_PER_GROUP + j] for g in range(N_GROUPS)])
             for j in range(EXPERTS_PER_GROUP)]
    w_lo = _select(lo, s_grp)
    w_hi = _select(hi, s_grp)
    tot = w_lo + w_hi
    return cls, w_lo / tot, w_hi / tot


def _merge_kernel(x_ref, a_ref, sg_ref, cv_ref, wg_ref, bg_ref, wa_ref, wb_ref, wc_ref, wo_ref, bo_ref,
                  g1_ref, b1_ref, wr_ref, br_ref, x1p_ref, route_ref, counts_ref, carry_ref):
    @pl.when(pl.program_id(0) == 0)
    def _():
        carry_ref[...] = jnp.zeros_like(carry_ref)

    x = x_ref[...]
    xb = x.astype(BF16)
    merged = None
    for i, (branch_ref, w_ref) in enumerate(((a_ref, wa_ref), (sg_ref, wb_ref), (cv_ref, wc_ref))):
        cols = slice(i * D_MODEL, (i + 1) * D_MODEL)
        gate = jax.nn.sigmoid(jnp.dot(xb, wg_ref[:, cols], preferred_element_type=F32) + bg_ref[:, cols])
        term = gate * jnp.dot(branch_ref[...], w_ref[...], preferred_element_type=F32)
        merged = term if merged is None else merged + term
    m = jnp.dot(merged.astype(BF16), wo_ref[...], preferred_element_type=F32) + bo_ref[...]
    x1 = _ln(ALPHA * x + m, g1_ref[...], b1_ref[...])
    x1p_ref[:, :D_MODEL] = x1

    nt = (((1,), (1,)), ((), ()))
    x_hi = x1.astype(BF16)
    x_lo = (x1 - x_hi.astype(F32)).astype(BF16)
    wr = wr_ref[...]
    w_hi = wr.astype(BF16)
    w_lo = (wr - w_hi.astype(F32)).astype(BF16)
    logits = (lax.dot_general(w_hi, x_hi, nt, preferred_element_type=F32)
              + lax.dot_general(w_hi, x_lo, nt, preferred_element_type=F32)
              + lax.dot_general(w_lo, x_hi, nt, preferred_element_type=F32))
    cls, ga, gb = _route(logits, br_ref[...])

    onehot = (lax.broadcasted_iota(jnp.int32, (N_CLASS_ROWS, TM), 0) == cls)
    oh_f = jnp.where(onehot, 1.0, 0.0)
    upper = (lax.broadcasted_iota(jnp.int32, (TM, TM), 0) <= lax.broadcasted_iota(jnp.int32, (TM, TM), 1))
    prefix = jnp.dot(oh_f.astype(BF16), jnp.where(upper, 1.0, 0.0).astype(BF16), preferred_element_type=F32)
    carry = carry_ref[:, 0:1]
    rank = jnp.sum(oh_f * (prefix - 1.0 + carry), axis=0, keepdims=True)
    new_carry = carry_ref[...] + jnp.sum(oh_f, axis=1, keepdims=True)
    carry_ref[...] = new_carry
    counts_ref[...] = new_carry

    rows = jnp.concatenate([cls.astype(F32), rank, ga, gb, jnp.zeros((4, TM), F32)], axis=0)
    route_ref[...] = rows
    x1p_ref[:, D_MODEL:] = jnp.concatenate([rows, jnp.zeros((LANES - 8, TM), F32)], axis=0).T


def _merge(x, a, sg, cv, w_gate, b_gate, w_a_o, w_b_o, w_c_o, w_o, b_o, ln_g, ln_b, w_router_t, b_router):
    n = x.shape[0]
    tok = lambda w: pl.BlockSpec((TM, w), lambda i: (i, 0))
    return pl.pallas_call(
        _merge_kernel,
        grid=(n // TM,),
        in_specs=[tok(D_MODEL), tok(NA_WIDTH), tok(SG_WIDTH), tok(CV_WIDTH),
                  _full((D_MODEL, 3 * D_MODEL)), _full((1, 3 * D_MODEL)),
                  _full((NA_WIDTH, D_MODEL)), _full((SG_WIDTH, D_MODEL)), _full((CV_WIDTH, D_MODEL)),
                  _full((D_MODEL, D_MODEL)), _full((1, D_MODEL)), _full((1, D_MODEL)), _full((1, D_MODEL)),
                  _full((N_EXPERTS, D_MODEL)), _full((N_EXPERTS, 1))],
        out_specs=[tok(PAYLOAD), pl.BlockSpec((None, 8, TM), lambda i: (i, 0, 0)),
                   _full((N_CLASS_ROWS, LANES))],
        out_shape=[jax.ShapeDtypeStruct((n, PAYLOAD), F32), jax.ShapeDtypeStruct((n // TM, 8, TM), F32),
                   jax.ShapeDtypeStruct((N_CLASS_ROWS, LANES), F32)],
        scratch_shapes=[pltpu.VMEM((N_CLASS_ROWS, LANES), F32)],
        compiler_params=_params("arbitrary"),
        name="merge_ln1_route",
    )(x, a, sg, cv, w_gate, b_gate, w_a_o, w_b_o, w_c_o, w_o, b_o, ln_g, ln_b, w_router_t, b_router)


def _dest_spec():
    return pl.BlockSpec((1, 1, TM), lambda i, *_: (i, 0, 0), memory_space=pltpu.SMEM)


def _scatter_rows_kernel(zero_tiles_ref, dest_ref, src_ref, dst_hbm, zeros_ref, sem):
    @pl.when(pl.program_id(0) == 0)
    def _():
        zeros_ref[...] = jnp.zeros_like(zeros_ref)
        fill = lambda c: pltpu.make_async_copy(
            zeros_ref, dst_hbm.at[pl.ds(pl.multiple_of(zero_tiles_ref[c] * TE, TE), TE)], sem)
        for c in range(N_CLASSES):
            fill(c).start()
        for c in range(N_CLASSES):
            fill(c).wait()

    def issue(t, carry):
        pltpu.make_async_copy(src_ref.at[pl.ds(t, 1)], dst_hbm.at[pl.ds(dest_ref[0, 0, t], 1)], sem).start()
        return carry

    lax.fori_loop(0, TM, issue, 0, unroll=8)
    pltpu.make_async_copy(src_ref, dst_hbm.at[pl.ds(0, TM)], sem).wait()


def _scatter_rows(zero_tiles, dest, src, n_rows_out):
    n, width = src.shape
    grid_spec = pltpu.PrefetchScalarGridSpec(
        num_scalar_prefetch=1,
        grid=(n // TM,),
        in_specs=[_dest_spec(), pl.BlockSpec((TM, width), lambda i, *_: (i, 0))],
        out_specs=pl.BlockSpec(memory_space=pl.ANY),
        scratch_shapes=[pltpu.VMEM((TE, width), src.dtype), pltpu.SemaphoreType.DMA(())],
    )
    return pl.pallas_call(
        _scatter_rows_kernel,
        grid_spec=grid_spec,
        out_shape=jax.ShapeDtypeStruct((n_rows_out, width), src.dtype),
        compiler_params=_params("arbitrary"),
        name="scatter_rows",
    )(zero_tiles, dest.reshape(n // TM, 1, TM), src)


def _gather_rows_kernel(dest_ref, src_hbm, o_ref, sem):
    def issue(t, carry):
        pltpu.make_async_copy(src_hbm.at[pl.ds(dest_ref[0, 0, t], 1)], o_ref.at[pl.ds(t, 1)], sem).start()
        return carry

    lax.fori_loop(0, TM, issue, 0, unroll=8)
    pltpu.make_async_copy(src_hbm.at[pl.ds(0, TM)], o_ref, sem).wait()


def _gather_rows(dest, src):
    n = dest.shape[0]
    width = src.shape[1]
    return pl.pallas_call(
        _gather_rows_kernel,
        grid=(n // TM,),
        in_specs=[_dest_spec(), pl.BlockSpec(memory_space=pl.ANY)],
        out_specs=pl.BlockSpec((TM, width), lambda i: (i, 0)),
        out_shape=jax.ShapeDtypeStruct((n, width), src.dtype),
        scratch_shapes=[pltpu.SemaphoreType.DMA(())],
        compiler_params=_params("arbitrary"),
        name="gather_rows",
    )(dest.reshape(n // TM, 1, TM), src)


def _expert_kernel(ea_ref, eb_ref, src_ref, valid_ref, xs_ref, w1a_ref, w3a_ref, w1b_ref, w3b_ref,
                   w2a_ref, w2b_ref, g_ref, b_ref, o_ref):
    del ea_ref, eb_ref, src_ref

    @pl.when(valid_ref[pl.program_id(0)] == 1)
    def _():
        x = xs_ref[:, :D_MODEL]
        xb = x.astype(BF16)
        ga = xs_ref[:, D_MODEL + COL_GA:D_MODEL + COL_GA + 1]
        gb = xs_ref[:, D_MODEL + COL_GB:D_MODEL + COL_GB + 1]

        def hidden(w1_ref, w3_ref, gate):
            a = jnp.dot(xb, w1_ref[...], preferred_element_type=F32)
            b = jnp.dot(xb, w3_ref[...], preferred_element_type=F32)
            return (a * jax.nn.sigmoid(a) * b * gate).astype(BF16)

        y = (jnp.dot(hidden(w1a_ref, w3a_ref, ga), w2a_ref[...], preferred_element_type=F32)
             + jnp.dot(hidden(w1b_ref, w3b_ref, gb), w2b_ref[...], preferred_element_type=F32))
        o_ref[...] = _ln(ALPHA * x + y, g_ref[...], b_ref[...])


def _experts(tile_ea, tile_eb, tile_src, tile_valid, xs, w_e1, w_e3, w_e2, ln_g, ln_b):
    n_tiles = xs.shape[0] // TE
    up = lambda which: pl.BlockSpec((None, D_MODEL, EXPERT_FF), lambda i, ea, eb, src, valid: ((ea, eb)[which][i], 0, 0))
    down = lambda which: pl.BlockSpec((None, EXPERT_FF, D_MODEL), lambda i, ea, eb, src, valid: ((ea, eb)[which][i], 0, 0))
    vec = pl.BlockSpec((1, D_MODEL), lambda i, *_: (0, 0))
    grid_spec = pltpu.PrefetchScalarGridSpec(
        num_scalar_prefetch=4,
        grid=(n_tiles,),
        in_specs=[pl.BlockSpec((TE, PAYLOAD), lambda i, ea, eb, src, valid: (src[i], 0)),
                  up(0), up(0), up(1), up(1), down(0), down(1), vec, vec],
        out_specs=pl.BlockSpec((TE, D_MODEL), lambda i, *_: (i, 0)),
    )
    return pl.pallas_call(
        _expert_kernel,
        grid_spec=grid_spec,
        out_shape=jax.ShapeDtypeStruct((n_tiles * TE, D_MODEL), F32),
        compiler_params=_params("arbitrary"),
        name="expert_pair_ln2",
    )(tile_ea, tile_eb, tile_src, tile_valid, xs, w_e1, w_e3, w_e1, w_e3, w_e2, w_e2, ln_g, ln_b)


def _routing_tables(route, counts, n):
    n_tiles = n // TE + N_CLASSES
    cls = route[:, COL_CLS, :].reshape(n).astype(jnp.int32)
    rank = route[:, COL_RANK, :].reshape(n).astype(jnp.int32)
    cnt = counts[:N_CLASSES, 0].astype(jnp.int32)
    padded = ((cnt + TE - 1) // TE) * TE
    ends = jnp.cumsum(padded)
    starts = ends - padded
    dest = starts[cls] + rank
    tile = jnp.arange(n_tiles, dtype=jnp.int32)
    n_used = ends[-1] // TE
    tile_valid = (tile < n_used).astype(jnp.int32)
    tile_src = jnp.minimum(tile, n_used - 1)
    tile_cls = jnp.sum((ends[None, :] <= (tile_src * TE)[:, None]).astype(jnp.int32), axis=1)
    tile_cls = jnp.minimum(tile_cls, N_CLASSES - 1)
    empty = cnt == 0
    zero_tiles = jnp.where(empty, n_tiles - jnp.cumsum(empty.astype(jnp.int32)), ends // TE - 1).astype(jnp.int32)
    return (dest, jnp.asarray(_CLASS_EA)[tile_cls], jnp.asarray(_CLASS_EB)[tile_cls], tile_src, tile_valid,
            zero_tiles)


def _trunk(x_a, x_b, emb_ln_g, emb_ln_b, w_in, b_in, w_gate, b_gate, rpb, w_a_o, sg_ln_g, sg_ln_b, w_s, b_s,
           w_b_o, conv_w, conv_b, cv_ln_g, cv_ln_b, w_c_o, w_o, b_o, ln1_g, ln1_b, w_router, b_router,
           w_e1, w_e3, w_e2, ln2_g, ln2_b):
    n = x_a.shape[0] + x_b.shape[0]
    row = lambda a: a.reshape(1, -1).astype(F32)
    bf = lambda a: a.astype(BF16)
    w_router_t = w_router.T.astype(F32)
    b_router_c = b_router.reshape(N_EXPERTS, 1).astype(F32)
    n_sorted = (n // TE + N_CLASSES) * TE

    dest = x2s = None
    for l in range(DEPTH):
        if l == 0:
            x, qkv, sg, cv = _embed_proj(x_a, x_b, row(emb_ln_g), row(emb_ln_b), bf(w_in[l]), row(b_in[l]))
        else:
            x, qkv, sg, cv = _gather_proj(dest, x2s, bf(w_in[l]), row(b_in[l]))
        a = _attention(qkv, _attn_bias_table(rpb[l]))
        bs_full = jnp.repeat(b_s[l].T.astype(F32), SG_WIDTH // SG_GROUPS, axis=1)
        sgo = _spatial_gate(sg, bf(w_s[l]), bs_full, row(sg_ln_g[l]), row(sg_ln_b[l]))
        cvo = _conv_module(cv, conv_w[l].astype(F32), row(conv_b[l]), row(cv_ln_g[l]), row(cv_ln_b[l]))
        x1p, route, counts = _merge(x, a, sgo, cvo, bf(w_gate[l]), row(b_gate[l]), bf(w_a_o[l]), bf(w_b_o[l]),
                                    bf(w_c_o[l]), bf(w_o[l]), row(b_o[l]), row(ln1_g[l]), row(ln1_b[l]),
                                    w_router_t, b_router_c)
        dest, tile_ea, tile_eb, tile_src, tile_valid, zero_tiles = _routing_tables(route, counts, n)
        xs = _scatter_rows(zero_tiles, dest, x1p, n_sorted)
        x2s = _experts(tile_ea, tile_eb, tile_src, tile_valid, xs, bf(w_e1[l]), bf(w_e3[l]), bf(w_e2[l]),
                       row(ln2_g[l]), row(ln2_b[l]))
    return dest, x2s


def kernel(x_prompt, x_sample, emb_ln_g, emb_ln_b, w_in, b_in, w_gate, b_gate, rpb, w_a_o, sg_ln_g, sg_ln_b,
           w_s, b_s, w_b_o, conv_w, conv_b, cv_ln_g, cv_ln_b, w_c_o, w_o, b_o, ln1_g, ln1_b, w_router,
           b_router, w_e1, w_e3, w_e2, ln2_g, ln2_b):
    weights = (emb_ln_g, emb_ln_b, w_in, b_in, w_gate, b_gate, rpb, w_a_o, sg_ln_g, sg_ln_b, w_s, b_s, w_b_o,
               conv_w, conv_b, cv_ln_g, cv_ln_b, w_c_o, w_o, b_o, ln1_g, ln1_b, w_router, b_router,
               w_e1, w_e3, w_e2, ln2_g, ln2_b)
    n_prompt = x_prompt.shape[0] * x_prompt.shape[1]
    dest, x_sorted = _trunk(x_prompt.reshape(-1, D_MODEL), x_sample.reshape(-1, D_MODEL), *weights)
    y_prompt = _gather_rows(dest[:n_prompt], x_sorted)
    y_sample = _gather_rows(dest[n_prompt:], x_sorted)
    return y_prompt.reshape(x_prompt.shape), y_sample.reshape(x_sample.shape)
```
